```python
import math
import jax, jax.numpy as jnp
from jax import lax
import numpy as np

D_MODEL = 1024
BATCH = 32
SEQ = 2048
DEPTH = 1

N_META = 16
SSM_WIDTH = 1024
SSM_GROUP = 16
SSM_GROUPS = SSM_WIDTH // SSM_GROUP
SSM_STATE = 64
DT_MIN = 1e-3
DT_MAX = 1e-1
HGRN_WIDTH = 1024
HGRN_HEAD_DIM = 128
HGRN_HEADS = HGRN_WIDTH // HGRN_HEAD_DIM
HGRN_CHUNK = 16
D_FF = 2816
CONV_WIDTH = 3
EPS = 1e-6
IN_COLS = SSM_WIDTH + 4 * HGRN_WIDTH + 2 * D_MODEL

kernel_name = 'hybrid_s5_hgrn2_gated_merge_block'


def rmsnorm(x, g):
    xf = x.astype(jnp.float32)
    y = xf * lax.rsqrt(jnp.mean(xf * xf, axis=-1, keepdims=True) + EPS)
    return (y * g.astype(jnp.float32)).astype(x.dtype)


def _complex_affine_combine(e1, e2):
    a1r, a1i, b1r, b1i = e1
    a2r, a2i, b2r, b2i = e2
    ar = a1r * a2r - a1i * a2i
    ai = a1r * a2i + a1i * a2r
    br = a2r * b1r - a2i * b1i + b2r
    bi = a2r * b1i + a2i * b1r + b2i
    return ar, ai, br, bi


def s5_mixer(u, lam_re, lam_im, log_dt, b_re, b_im, c_re, c_im, d_skip, w_glu):
    bsz, L, _ = u.shape
    uf = u.astype(jnp.float32).reshape(bsz, L, SSM_GROUPS, SSM_GROUP)
    lr = lam_re.astype(jnp.float32)
    li = lam_im.astype(jnp.float32)
    dt = jnp.exp(log_dt.astype(jnp.float32))[:, None]
    mag = jnp.exp(lr * dt)
    ab_re = mag * jnp.cos(li * dt)
    ab_im = mag * jnp.sin(li * dt)
    den = lr * lr + li * li
    nr = ab_re - 1.0
    coef_re = (nr * lr + ab_im * li) / den
    coef_im = (ab_im * lr - nr * li) / den
    br = b_re.astype(jnp.float32)
    bi = b_im.astype(jnp.float32)
    bb_re = coef_re[..., None] * br - coef_im[..., None] * bi
    bb_im = coef_re[..., None] * bi + coef_im[..., None] * br
    v_re = jnp.einsum('blgh,gph->blgp', uf, bb_re)
    v_im = jnp.einsum('blgh,gph->blgp', uf, bb_im)
    a_re = jnp.broadcast_to(ab_re[None, None], (1, L, SSM_GROUPS, SSM_STATE))
    a_im = jnp.broadcast_to(ab_im[None, None], (1, L, SSM_GROUPS, SSM_STATE))
    _, _, s_re, s_im = lax.associative_scan(_complex_affine_combine, (a_re, a_im, v_re, v_im), axis=1)
    y = (jnp.einsum('blgp,ghp->blgh', s_re, c_re.astype(jnp.float32))
         - jnp.einsum('blgp,ghp->blgh', s_im, c_im.astype(jnp.float32))
         + d_skip.astype(jnp.float32).reshape(SSM_GROUPS, SSM_GROUP) * uf)
    y = jax.nn.gelu(y.reshape(bsz, L, SSM_WIDTH)).astype(u.dtype)
    return y * jax.nn.sigmoid(y @ w_glu)


def hgrn2_mixer(q, f_logit, i_in, og, lb, norm_g):
    bsz, L, _ = q.shape
    n_chunks = L // HGRN_CHUNK

    def heads(t):
        return t.reshape(bsz, n_chunks, HGRN_CHUNK, HGRN_HEADS, HGRN_HEAD_DIM).transpose(1, 0, 3, 2, 4)

    lbf = lb.astype(jnp.float32)
    f = lbf + (1.0 - lbf) * jax.nn.sigmoid(f_logit.astype(jnp.float32))
    log_f = jnp.log(f)
    qh = heads(q.astype(jnp.float32))
    kh = heads(1.0 - f)
    vh = heads(i_in.astype(jnp.float32))
    cum = jnp.cumsum(heads(log_f), axis=3)
    q_in = qh * jnp.exp(cum)
    k_in = kh * jnp.exp(-cum)
    k_out = kh * jnp.exp(cum[..., -1:, :] - cum)
    chunk_decay = jnp.exp(cum[..., -1, :])
    causal = jnp.tril(jnp.ones((HGRN_CHUNK, HGRN_CHUNK), dtype=bool))

    def chunk_step(state, xs):
        qi, ki, ko, v, dec = xs
        scores = jnp.where(causal, jnp.einsum('bhtd,bhsd->bhts', qi, ki), 0.0)
        o = jnp.einsum('bhts,bhsv->bhtv', scores, v) + jnp.einsum('bhtd,bhdv->bhtv', qi, state)
        state = dec[..., None] * state + jnp.einsum('bhsd,bhsv->bhdv', ko, v)
        return state, o

    init = jnp.zeros((bsz, HGRN_HEADS, HGRN_HEAD_DIM, HGRN_HEAD_DIM), jnp.float32)
    _, o = lax.scan(chunk_step, init, (q_in, k_in, k_out, vh, chunk_decay))
    o = o.transpose(1, 0, 3, 2, 4).reshape(bsz, L, HGRN_HEADS, HGRN_HEAD_DIM)
    o = o * lax.rsqrt(jnp.mean(o * o, axis=-1, keepdims=True) + EPS) * norm_g.astype(jnp.float32)
    o = o.reshape(bsz, L, HGRN_WIDTH).astype(q.dtype)
    return o * jax.nn.silu(og)


def causal_dwconv(u, w, b):
    L = u.shape[1]
    up = jnp.pad(u, ((0, 0), (CONV_WIDTH - 1, 0), (0, 0)))
    out = b
    for j in range(CONV_WIDTH):
        out = out + up[:, j:j + L, :] * w[j]
    return out


def setup_inputs(seed: int = 0) -> dict:
    key = jax.random.key(seed)
    ks = jax.random.split(key, 24)
    f32 = jnp.float32

    def nrm(k, shape, scale):
        return jax.random.normal(k, shape, f32) * scale

    n_idx = jnp.arange(SSM_STATE, dtype=f32)
    return {
        'x': nrm(ks[0], (BATCH, SEQ, D_MODEL), 1.0),
        'meta_tokens': nrm(ks[1], (N_META, D_MODEL), 1.0),
        'mix_norm_g': 1.0 + nrm(ks[2], (DEPTH, D_MODEL), 0.02),
        'w_in': nrm(ks[3], (DEPTH, D_MODEL, IN_COLS), D_MODEL ** -0.5),
        'ssm_lambda_re': -0.5 + nrm(ks[4], (DEPTH, SSM_GROUPS, SSM_STATE), 0.01),
        'ssm_lambda_im': math.pi * n_idx + nrm(ks[5], (DEPTH, SSM_GROUPS, SSM_STATE), 0.01),
        'ssm_log_dt': jax.random.uniform(ks[6], (DEPTH, SSM_GROUPS), f32, math.log(DT_MIN), math.log(DT_MAX)),
        'ssm_b_re': nrm(ks[7], (DEPTH, SSM_GROUPS, SSM_STATE, SSM_GROUP), (2 * SSM_GROUP) ** -0.5),
        'ssm_b_im': nrm(ks[8], (DEPTH, SSM_GROUPS, SSM_STATE, SSM_GROUP), (2 * SSM_GROUP) ** -0.5),
        'ssm_c_re': nrm(ks[9], (DEPTH, SSM_GROUPS, SSM_GROUP, SSM_STATE), SSM_STATE ** -0.5),
        'ssm_c_im': nrm(ks[10], (DEPTH, SSM_GROUPS, SSM_GROUP, SSM_STATE), SSM_STATE ** -0.5),
        'ssm_d': nrm(ks[11], (DEPTH, SSM_WIDTH), 1.0),
        'ssm_w_glu': nrm(ks[12], (DEPTH, SSM_WIDTH, SSM_WIDTH), SSM_WIDTH ** -0.5),
        'w_ssm_proj': nrm(ks[13], (DEPTH, SSM_WIDTH, D_MODEL), SSM_WIDTH ** -0.5),
        'hgrn_lb_logits': nrm(ks[14], (DEPTH + 1, HGRN_WIDTH), 0.1),
        'hgrn_norm_g': 1.0 + nrm(ks[15], (DEPTH, HGRN_HEAD_DIM), 0.02),
        'w_hgrn_proj': nrm(ks[16], (DEPTH, HGRN_WIDTH, D_MODEL), HGRN_WIDTH ** -0.5),
        'w_out': nrm(ks[17], (DEPTH, D_MODEL, D_MODEL), D_MODEL ** -0.5),
        'ffn_norm_g': 1.0 + nrm(ks[18], (DEPTH, D_MODEL), 0.02),
        'w_up': nrm(ks[19], (DEPTH, D_MODEL, 2 * D_FF), D_MODEL ** -0.5),
        'conv_w': nrm(ks[20], (DEPTH, CONV_WIDTH, 2 * D_FF), CONV_WIDTH ** -0.5),
        'conv_b': nrm(ks[21], (DEPTH, 2 * D_FF), 0.01),
        'w_down': nrm(ks[22], (DEPTH, D_FF, D_MODEL), D_FF ** -0.5),
        'final_norm_g': 1.0 + nrm(ks[23], (D_MODEL,), 0.02),
    }


def reference(x, meta_tokens, mix_norm_g, w_in, ssm_lambda_re, ssm_lambda_im, ssm_log_dt,
              ssm_b_re, ssm_b_im, ssm_c_re, ssm_c_im, ssm_d, ssm_w_glu, w_ssm_proj,
              hgrn_lb_logits, hgrn_norm_g, w_hgrn_proj, w_out, ffn_norm_g, w_up, conv_w,
              conv_b, w_down, final_norm_g):
    bsz = x.shape[0]
    meta = jnp.broadcast_to(meta_tokens.astype(x.dtype)[None], (bsz, N_META, D_MODEL))
    h = jnp.concatenate([meta, x], axis=1)
    lower_bounds = jnp.cumsum(jax.nn.softmax(hgrn_lb_logits.astype(jnp.float32), axis=0), axis=0)
    o1 = SSM_WIDTH
    o2 = o1 + HGRN_WIDTH
    o3 = o2 + HGRN_WIDTH
    o4 = o3 + HGRN_WIDTH
    o5 = o4 + HGRN_WIDTH
    o6 = o5 + D_MODEL
    for l in range(DEPTH):
        z = rmsnorm(h, mix_norm_g[l])
        p = z @ w_in[l]
        y_a = s5_mixer(p[..., :o1], ssm_lambda_re[l], ssm_lambda_im[l], ssm_log_dt[l],
                       ssm_b_re[l], ssm_b_im[l], ssm_c_re[l], ssm_c_im[l], ssm_d[l], ssm_w_glu[l])
        y_b = hgrn2_mixer(p[..., o1:o2], p[..., o2:o3], p[..., o3:o4], p[..., o4:o5],
                          lower_bounds[l], hgrn_norm_g[l])
        merged = (jax.nn.sigmoid(p[..., o5:o6]) * (y_a @ w_ssm_proj[l])
                  + jax.nn.sigmoid(p[..., o6:]) * (y_b @ w_hgrn_proj[l]))
        h = h + merged @ w_out[l]
        z = rmsnorm(h, ffn_norm_g[l])
        u = causal_dwconv(z @ w_up[l], conv_w[l], conv_b[l])
        h = h + (jax.nn.silu(u[..., :D_FF]) * u[..., D_FF:]) @ w_down[l]
    return rmsnorm(h[:, N_META:], final_norm_g)
```

```python
import functools
import math

import jax
import jax.numpy as jnp
from jax import lax
from jax.experimental import pallas as pl
from jax.experimental.pallas import tpu as pltpu

F32 = jnp.float32
BF16 = jnp.bfloat16

D_MODEL = 1024
N_META = 16
SSM_GROUPS = 64
SSM_GROUP = 16
SSM_STATE = 64
GROUPS_PER_BLOCK = 16
N_SSM_BLOCKS = SSM_GROUPS // GROUPS_PER_BLOCK
SSM_BLOCK_IN = GROUPS_PER_BLOCK * SSM_GROUP
SSM_BLOCK_STATE = GROUPS_PER_BLOCK * SSM_STATE
HGRN_HEADS = 8
HGRN_HEAD_DIM = 128
HGRN_CHUNK = 32
D_FF = 2816
FF_CHUNK = 256
N_FF_CHUNKS = D_FF // FF_CHUNK
CONV_WIDTH = 3
IN_BLOCKS = 7
EPS = 1e-6
LANES = 128
VMEM_LIMIT_BYTES = 56 * 1024 * 1024


def _rmsnorm(x, g):
    return x * lax.rsqrt(jnp.mean(x * x, axis=-1, keepdims=True) + EPS) * g


def _resident(shape):
    n = len(shape)
    return pl.BlockSpec(shape, lambda *_: (0,) * n, pipeline_mode=pl.Buffered(1))


def _params(semantics):
    return pltpu.CompilerParams(dimension_semantics=semantics, vmem_limit_bytes=VMEM_LIMIT_BYTES)


def _inproj_kernel(x_ref, g_ref, w_ref, p_ref):
    z = _rmsnorm(x_ref[...], g_ref[...]).astype(BF16)
    for k in range(IN_BLOCKS):
        cols = slice(k * D_MODEL, (k + 1) * D_MODEL)
        p_ref[:, cols] = jnp.dot(z, w_ref[:, cols], preferred_element_type=F32)


def _inproj(x, g, w, tt):
    b, l, _ = x.shape
    return pl.pallas_call(
        _inproj_kernel,
        grid=(b, l // tt),
        in_specs=[
            pl.BlockSpec((None, tt, D_MODEL), lambda i, t: (i, t, 0)),
            _resident((1, D_MODEL)),
            _resident((D_MODEL, IN_BLOCKS * D_MODEL)),
        ],
        out_specs=pl.BlockSpec((tt, IN_BLOCKS * D_MODEL), lambda i, t: (t, i)),
        out_shape=jax.ShapeDtypeStruct((l, b * IN_BLOCKS * D_MODEL), F32),
        compiler_params=_params(("parallel", "parallel")),
        name="inproj",
    )(x, g, w)


def _s5_kernel(u_ref, bb_ref, cc_ref, are_ref, aim_ref, d_ref, wglu_ref, s0_ref,
               ya_ref, sfin_ref, st_ref, v_ref, y_ref, *, tt, nb):
    ti = pl.program_id(0)

    @pl.when(ti == 0)
    def _():
        st_ref[...] = s0_ref[...]

    rows = tt * nb
    u = u_ref[...].reshape(rows, D_MODEL)
    for j in range(N_SSM_BLOCKS):
        uj = u[:, j * SSM_BLOCK_IN:(j + 1) * SSM_BLOCK_IN].astype(BF16)
        v_ref[...] = jnp.dot(uj, bb_ref[j], preferred_element_type=F32)
        for c in range(SSM_BLOCK_STATE // LANES):
            re_cols = slice(c * LANES, (c + 1) * LANES)
            im_cols = slice(SSM_BLOCK_STATE + c * LANES, SSM_BLOCK_STATE + (c + 1) * LANES)
            a_cols = slice(j * SSM_BLOCK_STATE + c * LANES, j * SSM_BLOCK_STATE + (c + 1) * LANES)
            sre_cols = slice(2 * j * SSM_BLOCK_STATE + c * LANES, 2 * j * SSM_BLOCK_STATE + (c + 1) * LANES)
            sim_cols = slice((2 * j + 1) * SSM_BLOCK_STATE + c * LANES,
                             (2 * j + 1) * SSM_BLOCK_STATE + (c + 1) * LANES)
            ar = jnp.broadcast_to(are_ref[:, a_cols], (nb, LANES))
            ai = jnp.broadcast_to(aim_ref[:, a_cols], (nb, LANES))

            def step(t, carry, re_cols=re_cols, im_cols=im_cols, ar=ar, ai=ai):
                sr, si = carry
                r = pl.ds(pl.multiple_of(t * nb, nb), nb)
                nr = ar * sr - ai * si + v_ref[r, re_cols]
                ni = ar * si + ai * sr + v_ref[r, im_cols]
                v_ref[r, re_cols] = nr
                v_ref[r, im_cols] = ni
                return nr, ni

            sr, si = lax.fori_loop(0, tt, step, (st_ref[:, sre_cols], st_ref[:, sim_cols]))
            st_ref[:, sre_cols] = sr
            st_ref[:, sim_cols] = si
        y_ref[:, j * SSM_BLOCK_IN:(j + 1) * SSM_BLOCK_IN] = jnp.dot(
            v_ref[...].astype(BF16), cc_ref[j], preferred_element_type=F32)

    y = jax.nn.gelu(y_ref[...] + d_ref[...] * u)
    gate = jnp.dot(y.astype(BF16), wglu_ref[...], preferred_element_type=F32)
    ya_ref[...] = (y * jax.nn.sigmoid(gate)).reshape(tt, nb, D_MODEL)

    @pl.when(ti == pl.num_programs(0) - 1)
    def _():
        sfin_ref[...] = st_ref[...]


def _s5(p3d, bb, cc, a_re, a_im, d_skip, w_glu, s0, tt):
    l, nb, _ = p3d.shape
    n_state = 2 * SSM_GROUPS * SSM_STATE
    return pl.pallas_call(
        functools.partial(_s5_kernel, tt=tt, nb=nb),
        grid=(l // tt,),
        in_specs=[
            pl.BlockSpec((tt, nb, D_MODEL), lambda t: (t, 0, 0)),
            _resident(bb.shape),
            _resident(cc.shape),
            _resident(a_re.shape),
            _resident(a_im.shape),
            _resident((1, D_MODEL)),
            _resident((D_MODEL, D_MODEL)),
            _resident((nb, n_state)),
        ],
        out_specs=[
            pl.BlockSpec((tt, nb, D_MODEL), lambda t: (t, 0, 0)),
            pl.BlockSpec((nb, n_state), lambda t: (0, 0)),
        ],
        out_shape=[
            jax.ShapeDtypeStruct((l, nb, D_MODEL), F32),
            jax.ShapeDtypeStruct((nb, n_state), F32),
        ],
        scratch_shapes=[
            pltpu.VMEM((nb, n_state), F32),
            pltpu.VMEM((tt * nb, 2 * SSM_BLOCK_STATE), F32),
            pltpu.VMEM((tt * nb, D_MODEL), F32),
        ],
        compiler_params=_params(("arbitrary",)),
        name="s5",
    )(p3d, bb, cc, a_re, a_im, d_skip, w_glu, s0)


def _hgrn_kernel(q_ref, f_ref, i_ref, og_ref, lb_ref, ng_ref, s0_ref,
                 yb_ref, sfin_ref, st_ref, *, tt, chunk):
    ti = pl.program_id(1)

    @pl.when(ti == 0)
    def _():
        st_ref[...] = s0_ref[...]

    t_idx = lax.broadcasted_iota(jnp.int32, (chunk, chunk), 0)
    s_idx = lax.broadcasted_iota(jnp.int32, (chunk, chunk), 1)
    causal = s_idx <= t_idx
    tril = causal.astype(BF16)
    lb = lb_ref[...]
    mid = chunk // 2 - 1

    def chunk_step(ci, carry):
        r = pl.ds(pl.multiple_of(ci * chunk, chunk), chunk)
        f = lb + (1.0 - lb) * jax.nn.sigmoid(f_ref[r, :])
        log_f = jnp.log(f)
        k = 1.0 - f
        hi = log_f.astype(BF16)
        rem = log_f - hi.astype(F32)
        md = rem.astype(BF16)
        lo = (rem - md.astype(F32)).astype(BF16)
        cum = (jnp.dot(tril, hi, preferred_element_type=F32)
               + jnp.dot(tril, md, preferred_element_type=F32)
               + jnp.dot(tril, lo, preferred_element_type=F32))
        ref_row = cum[mid:mid + 1, :]
        last = cum[chunk - 1:chunk, :]
        q = q_ref[r, :]
        q_mid = (q * jnp.exp(cum - ref_row)).astype(BF16)
        k_mid = (k * jnp.exp(ref_row - cum)).astype(BF16)
        q_in = (q * jnp.exp(cum)).astype(BF16)
        k_out = (k * jnp.exp(last - cum)).astype(BF16)
        decay = jnp.exp(last)
        v = i_ref[r, :].astype(BF16)
        og = og_ref[r, :]
        for h in range(HGRN_HEADS):
            sl = slice(h * HGRN_HEAD_DIM, (h + 1) * HGRN_HEAD_DIM)
            scores = lax.dot_general(q_mid[:, sl], k_mid[:, sl], (((1,), (1,)), ((), ())),
                                     preferred_element_type=F32)
            scores = jnp.where(causal, scores, 0.0).astype(BF16)
            state_t = st_ref[h]
            o = (jnp.dot(scores, v[:, sl], preferred_element_type=F32)
                 + lax.dot_general(q_in[:, sl], state_t.astype(BF16), (((1,), (1,)), ((), ())),
                                   preferred_element_type=F32))
            st_ref[h] = state_t * decay[:, sl] + lax.dot_general(
                v[:, sl], k_out[:, sl], (((0,), (0,)), ((), ())), preferred_element_type=F32)
            o = _rmsnorm(o, ng_ref[:, sl])
            yb_ref[r, sl] = o * (og[:, sl] * jax.nn.sigmoid(og[:, sl]))
        return carry

    lax.fori_loop(0, tt // chunk, chunk_step, 0)

    @pl.when(ti == pl.num_programs(1) - 1)
    def _():
        sfin_ref[...] = st_ref[...]


def _hgrn(p2d, lb, ng, s0, nb, tt, chunk):
    l = p2d.shape[0]
    col = lambda k: pl.BlockSpec((tt, D_MODEL), lambda i, t, k=k: (t, i * IN_BLOCKS + k))
    state_spec = pl.BlockSpec((None, HGRN_HEADS, HGRN_HEAD_DIM, HGRN_HEAD_DIM), lambda i, t: (i, 0, 0, 0))
    return pl.pallas_call(
        functools.partial(_hgrn_kernel, tt=tt, chunk=chunk),
        grid=(nb, l // tt),
        in_specs=[col(1), col(2), col(3), col(4), _resident((1, D_MODEL)), _resident((1, D_MODEL)),
                  state_spec],
        out_specs=[pl.BlockSpec((tt, D_MODEL), lambda i, t: (t, i)), state_spec],
        out_shape=[
            jax.ShapeDtypeStruct((l, nb * D_MODEL), F32),
            jax.ShapeDtypeStruct((nb, HGRN_HEADS, HGRN_HEAD_DIM, HGRN_HEAD_DIM), F32),
        ],
        scratch_shapes=[pltpu.VMEM((HGRN_HEADS, HGRN_HEAD_DIM, HGRN_HEAD_DIM), F32)],
        compiler_params=_params(("parallel", "arbitrary")),
        name="hgrn2",
    )(p2d, p2d, p2d, p2d, lb, ng, s0)


def _merge_kernel(x_ref, ga_ref, gb_ref, ya_ref, yb_ref, wa_ref, wb_ref, wo_ref, h_ref):
    a = jnp.dot(ya_ref[...].astype(BF16), wa_ref[...], preferred_element_type=F32)
    b = jnp.dot(yb_ref[...].astype(BF16), wb_ref[...], preferred_element_type=F32)
    merged = jax.nn.sigmoid(ga_ref[...]) * a + jax.nn.sigmoid(gb_ref[...]) * b
    h_ref[...] = x_ref[...] + jnp.dot(merged.astype(BF16), wo_ref[...], preferred_element_type=F32)


def _merge(x, p2d, ya2d, yb2d, wa, wb, wo, tt):
    b, l, _ = x.shape
    row = pl.BlockSpec((tt, D_MODEL), lambda i, t: (t, i))
    col = lambda k: pl.BlockSpec((tt, D_MODEL), lambda i, t, k=k: (t, i * IN_BLOCKS + k))
    w = _resident((D_MODEL, D_MODEL))
    return pl.pallas_call(
        _merge_kernel,
        grid=(b, l // tt),
        in_specs=[pl.BlockSpec((None, tt, D_MODEL), lambda i, t: (i, t, 0)), col(5), col(6), row, row,
                  w, w, w],
        out_specs=row,
        out_shape=jax.ShapeDtypeStruct((l, b * D_MODEL), F32),
        compiler_params=_params(("parallel", "parallel")),
        name="merge",
    )(x, p2d, p2d, ya2d, yb2d, wa, wb, wo)


def _ffn_kernel(h_ref, g_ref, wg_ref, wv_ref, cwg_ref, cwv_ref, cbg_ref, cbv_ref, wd_ref, gf_ref,
                c0_ref, o_ref, cfin_ref, carry_ref, acc_ref, *, tt, nb):
    ti = pl.program_id(0)

    @pl.when(ti == 0)
    def _():
        carry_ref[...] = c0_ref[...]

    rows = tt * nb
    tail = (CONV_WIDTH - 1) * nb
    h = h_ref[...].reshape(rows, D_MODEL)
    z = _rmsnorm(h, g_ref[...]).astype(BF16)

    def conv(pre, carry_idx, cw, cb):
        ext = jnp.concatenate([carry_ref[carry_idx], pre], axis=0)
        carry_ref[carry_idx] = ext[rows:rows + tail]
        out = cb
        for j in range(CONV_WIDTH):
            out = out + ext[j * nb:j * nb + rows] * cw[j:j + 1]
        return out

    for n in range(N_FF_CHUNKS):
        gate = conv(jnp.dot(z, wg_ref[n], preferred_element_type=F32), n, cwg_ref[n], cbg_ref[n])
        val = conv(jnp.dot(z, wv_ref[n], preferred_element_type=F32), N_FF_CHUNKS + n,
                   cwv_ref[n], cbv_ref[n])
        act = (gate * jax.nn.sigmoid(gate) * val).astype(BF16)
        part = jnp.dot(act, wd_ref[n], preferred_element_type=F32)
        if n == 0:
            acc_ref[...] = part
        else:
            acc_ref[...] += part

    out = _rmsnorm(h + acc_ref[...], gf_ref[...])
    for t in range(tt):
        o_ref[:, t * D_MODEL:(t + 1) * D_MODEL] = out[t * nb:(t + 1) * nb]

    @pl.when(ti == pl.num_programs(0) - 1)
    def _():
        cfin_ref[...] = carry_ref[...]


def _ffn(h3d, g, wg, wv, cwg, cwv, cbg, cbv, wd, gf, c0, tt):
    l, nb, _ = h3d.shape
    carry_shape = (2 * N_FF_CHUNKS, (CONV_WIDTH - 1) * nb, FF_CHUNK)
    return pl.pallas_call(
        functools.partial(_ffn_kernel, tt=tt, nb=nb),
        grid=(l // tt,),
        in_specs=[
            pl.BlockSpec((tt, nb, D_MODEL), lambda t: (t, 0, 0)),
            _resident((1, D_MODEL)),
            _resident(wg.shape), _resident(wv.shape),
            _resident(cwg.shape), _resident(cwv.shape), _resident(cbg.shape), _resident(cbv.shape),
            _resident(wd.shape),
            _resident((1, D_MODEL)),
            _resident(carry_shape),
        ],
        out_specs=[
            pl.BlockSpec((nb, tt * D_MODEL), lambda t: (0, t)),
            pl.BlockSpec(carry_shape, lambda t: (0, 0, 0)),
        ],
        out_shape=[
            jax.ShapeDtypeStruct((nb, l * D_MODEL), F32),
            jax.ShapeDtypeStruct(carry_shape, F32),
        ],
        scratch_shapes=[
            pltpu.VMEM(carry_shape, F32),
            pltpu.VMEM((tt * nb, D_MODEL), F32),
        ],
        compiler_params=_params(("arbitrary",)),
        name="ffn",
    )(h3d, g, wg, wv, cwg, cwv, cbg, cbv, wd, gf, c0)


def _s5_params(lam_re, lam_im, log_dt, b_re, b_im, c_re, c_im):
    dt = jnp.exp(log_dt)[:, None]
    mag = jnp.exp(lam_re * dt)
    ab_re = mag * jnp.cos(lam_im * dt)
    ab_im = mag * jnp.sin(lam_im * dt)
    den = lam_re * lam_re + lam_im * lam_im
    nr = ab_re - 1.0
    coef_re = (nr * lam_re + ab_im * lam_im) / den
    coef_im = (ab_im * lam_re - nr * lam_im) / den
    bb_re = coef_re[..., None] * b_re - coef_im[..., None] * b_im
    bb_im = coef_re[..., None] * b_im + coef_im[..., None] * b_re
    eye = jnp.eye(GROUPS_PER_BLOCK, dtype=F32)

    def pack_in(m):
        m = m.reshape(N_SSM_BLOCKS, GROUPS_PER_BLOCK, SSM_STATE, SSM_GROUP)
        return jnp.einsum("jgph,gk->jghkp", m, eye).reshape(N_SSM_BLOCKS, SSM_BLOCK_IN, SSM_BLOCK_STATE)

    def pack_out(m):
        m = m.reshape(N_SSM_BLOCKS, GROUPS_PER_BLOCK, SSM_GROUP, SSM_STATE)
        return jnp.einsum("jghp,gk->jgpkh", m, eye).reshape(N_SSM_BLOCKS, SSM_BLOCK_STATE, SSM_BLOCK_IN)

    bb = jnp.concatenate([pack_in(bb_re), pack_in(bb_im)], axis=2).astype(BF16)
    cc = jnp.concatenate([pack_out(c_re), -pack_out(c_im)], axis=1).astype(BF16)
    return bb, cc, ab_re.reshape(1, -1), ab_im.reshape(1, -1)


def _ff_chunks(a, axis):
    a = jnp.moveaxis(a, axis, -1)
    a = a.reshape(a.shape[:-1] + (N_FF_CHUNKS, FF_CHUNK))
    return jnp.moveaxis(a, -2, 0)


def _block(h, states, w, tiles):
    nb, l, _ = h.shape
    s5_state, hgrn_state, conv_carry = states
    t_in, t_s5, t_hgrn, t_merge, t_ffn = (min(t, l) for t in tiles)
    p2d = _inproj(h, w["mix_g"], w["w_in"], t_in)
    p3d = p2d.reshape(l, nb, IN_BLOCKS * D_MODEL)
    ya, s5_state = _s5(p3d, w["bb"], w["cc"], w["a_re"], w["a_im"], w["d_skip"], w["w_glu"], s5_state, t_s5)
    yb2d, hgrn_state = _hgrn(p2d, w["lb"], w["hgrn_g"], hgrn_state, nb, t_hgrn, min(HGRN_CHUNK, l))
    h1 = _merge(h, p2d, ya.reshape(l, nb * D_MODEL), yb2d, w["w_a"], w["w_b"], w["w_out"], t_merge)
    out, conv_carry = _ffn(h1.reshape(l, nb, D_MODEL), w["ffn_g"], w["w_up_g"], w["w_up_v"],
                           w["cw_g"], w["cw_v"], w["cb_g"], w["cb_v"], w["w_down"], w["final_g"],
                           conv_carry, t_ffn)
    return out.reshape(nb, l, D_MODEL), (s5_state, hgrn_state, conv_carry)


def kernel(x, meta_tokens, mix_norm_g, w_in, ssm_lambda_re, ssm_lambda_im, ssm_log_dt, ssm_b_re, ssm_b_im, ssm_c_re, ssm_c_im, ssm_d, ssm_w_glu, w_ssm_proj, hgrn_lb_logits, hgrn_norm_g, w_hgrn_proj, w_out, ffn_norm_g, w_up, conv_w, conv_b, w_down, final_norm_g):
    nb = x.shape[0]
    bb, cc, a_re, a_im = _s5_params(ssm_lambda_re[0], ssm_lambda_im[0], ssm_log_dt[0],
                                    ssm_b_re[0], ssm_b_im[0], ssm_c_re[0], ssm_c_im[0])
    lower_bounds = jnp.cumsum(jax.nn.softmax(hgrn_lb_logits.astype(F32), axis=0), axis=0)
    w = {
        "mix_g": mix_norm_g[0].reshape(1, D_MODEL),
        "w_in": w_in[0].astype(BF16),
        "bb": bb, "cc": cc, "a_re": a_re, "a_im": a_im,
        "d_skip": ssm_d[0].reshape(1, D_MODEL),
        "w_glu": ssm_w_glu[0].astype(BF16),
        "lb": lower_bounds[0].reshape(1, D_MODEL),
        "hgrn_g": jnp.tile(hgrn_norm_g[0], HGRN_HEADS).reshape(1, D_MODEL),
        "w_a": w_ssm_proj[0].astype(BF16),
        "w_b": w_hgrn_proj[0].astype(BF16),
        "w_out": w_out[0].astype(BF16),
        "ffn_g": ffn_norm_g[0].reshape(1, D_MODEL),
        "w_up_g": _ff_chunks(w_up[0][:, :D_FF], 1).astype(BF16),
        "w_up_v": _ff_chunks(w_up[0][:, D_FF:], 1).astype(BF16),
        "cw_g": _ff_chunks(conv_w[0][:, :D_FF], 1),
        "cw_v": _ff_chunks(conv_w[0][:, D_FF:], 1),
        "cb_g": _ff_chunks(conv_b[0][None, :D_FF], 1),
        "cb_v": _ff_chunks(conv_b[0][None, D_FF:], 1),
        "w_down": w_down[0].reshape(N_FF_CHUNKS, FF_CHUNK, D_MODEL).astype(BF16),
        "final_g": final_norm_g.reshape(1, D_MODEL),
    }
    zero_states = (
        jnp.zeros((nb, 2 * SSM_GROUPS * SSM_STATE), F32),
        jnp.zeros((nb, HGRN_HEADS, HGRN_HEAD_DIM, HGRN_HEAD_DIM), F32),
        jnp.zeros((2 * N_FF_CHUNKS, (CONV_WIDTH - 1) * nb, FF_CHUNK), F32),
    )
    tiles = (256, 16, 256, 512, 16)
    meta = jnp.broadcast_to(meta_tokens.astype(x.dtype)[None], (nb, N_META, D_MODEL))
    _, states = _block(meta, zero_states, w, tiles)
    out, _ = _block(x, states, w, tiles)
    return out
```

```python
import functools

import jax
import jax.numpy as jnp
from jax import lax
from jax.experimental import pallas as pl
from jax.experimental.pallas import tpu as pltpu

F32 = jnp.float32
BF16 = jnp.bfloat16

D_MODEL = 1024
N_META = 16
SSM_GROUPS = 64
SSM_GROUP = 16
SSM_STATE = 64
GROUPS_PER_BLOCK = 16
N_SSM_BLOCKS = SSM_GROUPS // GROUPS_PER_BLOCK
SSM_BLOCK_IN = GROUPS_PER_BLOCK * SSM_GROUP
SSM_BLOCK_STATE = GROUPS_PER_BLOCK * SSM_STATE
N_SSM_STATE = 2 * SSM_GROUPS * SSM_STATE
HGRN_HEADS = 8
HGRN_HEAD_DIM = 128
HGRN_CHUNK = 32
HGRN_HEADS_PER_STEP = 2
D_FF = 2816
FF_CHUNK = 256
N_FF_CHUNKS = D_FF // FF_CHUNK
CONV_WIDTH = 3
IN_BLOCKS = 7
EPS = 1e-6
LANES = 128
SUBLANES = 8
N_LANE_SLABS = D_MODEL // LANES
VMEM_LIMIT_BYTES = 56 * 1024 * 1024
TILES = (32, 64, 256, 64, 64)


def _rmsnorm(x, g):
    return x * lax.rsqrt(jnp.mean(x * x, axis=-1, keepdims=True) + EPS) * g


def _resident(shape):
    n = len(shape)
    return pl.BlockSpec(shape, lambda *_: (0,) * n, pipeline_mode=pl.Buffered(1))


def _params(semantics):
    return pltpu.CompilerParams(dimension_semantics=semantics, vmem_limit_bytes=VMEM_LIMIT_BYTES)


def _rows(tt, cols, k=0):
    return pl.BlockSpec((None, tt * SUBLANES, cols), lambda g, t, k=k: (g, t, k))


def _inproj_kernel(x_ref, g_ref, w_ref, p_ref, h_ref, slab_ref, *, tt):
    for b in range(SUBLANES):
        for j in range(N_LANE_SLABS):
            slab_ref[j, pl.ds(b, tt, stride=SUBLANES), :] = x_ref[b, :, j * LANES:(j + 1) * LANES]
    h = jnp.concatenate([slab_ref[j] for j in range(N_LANE_SLABS)], axis=1)
    h_ref[...] = h
    z = _rmsnorm(h, g_ref[...]).astype(BF16)
    for k in range(IN_BLOCKS):
        cols = slice(k * D_MODEL, (k + 1) * D_MODEL)
        p_ref[:, cols] = jnp.dot(z, w_ref[:, cols], preferred_element_type=F32)


def _inproj(x, g, w, tt):
    b, l, _ = x.shape
    ng = b // SUBLANES
    return pl.pallas_call(
        functools.partial(_inproj_kernel, tt=tt),
        grid=(ng, l // tt),
        in_specs=[
            pl.BlockSpec((SUBLANES, tt, D_MODEL), lambda i, t: (i, t, 0)),
            _resident((1, D_MODEL)),
            _resident((D_MODEL, IN_BLOCKS * D_MODEL)),
        ],
        out_specs=[_rows(tt, IN_BLOCKS * D_MODEL), _rows(tt, D_MODEL)],
        out_shape=[
            jax.ShapeDtypeStruct((ng, l * SUBLANES, IN_BLOCKS * D_MODEL), F32),
            jax.ShapeDtypeStruct((ng, l * SUBLANES, D_MODEL), F32),
        ],
        scratch_shapes=[pltpu.VMEM((N_LANE_SLABS, tt * SUBLANES, LANES), F32)],
        compiler_params=_params(("parallel", "parallel")),
        name="inproj",
    )(x, g, w)


def _s5_kernel(u_ref, bb_ref, cc_ref, are_ref, aim_ref, d_ref, wglu_ref, s0_ref,
               ya_ref, sfin_ref, st_ref, v_ref, y_ref, *, tt):
    ti = pl.program_id(1)

    @pl.when(ti == 0)
    def _():
        st_ref[...] = s0_ref[...]

    u = u_ref[...]
    for j in range(N_SSM_BLOCKS):
        uj = u[:, j * SSM_BLOCK_IN:(j + 1) * SSM_BLOCK_IN].astype(BF16)
        v_ref[...] = jnp.dot(uj, bb_ref[j], preferred_element_type=F32)
        re_cols = slice(0, SSM_BLOCK_STATE)
        im_cols = slice(SSM_BLOCK_STATE, 2 * SSM_BLOCK_STATE)
        a_cols = slice(j * SSM_BLOCK_STATE, (j + 1) * SSM_BLOCK_STATE)
        sre_cols = slice(2 * j * SSM_BLOCK_STATE, (2 * j + 1) * SSM_BLOCK_STATE)
        sim_cols = slice((2 * j + 1) * SSM_BLOCK_STATE, (2 * j + 2) * SSM_BLOCK_STATE)
        ar = jnp.broadcast_to(are_ref[:, a_cols], (SUBLANES, SSM_BLOCK_STATE))
        ai = jnp.broadcast_to(aim_ref[:, a_cols], (SUBLANES, SSM_BLOCK_STATE))

        def step(t, carry, ar=ar, ai=ai):
            sr, si = carry
            r = pl.ds(pl.multiple_of(t * SUBLANES, SUBLANES), SUBLANES)
            nr = ar * sr - ai * si + v_ref[r, re_cols]
            ni = ar * si + ai * sr + v_ref[r, im_cols]
            v_ref[r, re_cols] = nr
            v_ref[r, im_cols] = ni
            return nr, ni

        sr, si = lax.fori_loop(0, tt, step, (st_ref[:, sre_cols], st_ref[:, sim_cols]))
        st_ref[:, sre_cols] = sr
        st_ref[:, sim_cols] = si
        y_ref[:, j * SSM_BLOCK_IN:(j + 1) * SSM_BLOCK_IN] = jnp.dot(
            v_ref[...].astype(BF16), cc_ref[j], preferred_element_type=F32)

    y = jax.nn.gelu(y_ref[...] + d_ref[...] * u)
    gate = jnp.dot(y.astype(BF16), wglu_ref[...], preferred_element_type=F32)
    ya_ref[...] = y * jax.nn.sigmoid(gate)

    @pl.when(ti == pl.num_programs(1) - 1)
    def _():
        sfin_ref[...] = st_ref[...]


def _s5(p, bb, cc, a_re, a_im, d_skip, w_glu, s0, tt):
    ng, rows, _ = p.shape
    state_shape = (SUBLANES, N_SSM_STATE)
    return pl.pallas_call(
        functools.partial(_s5_kernel, tt=tt),
        grid=(ng, rows // (tt * SUBLANES)),
        in_specs=[
            _rows(tt, D_MODEL, 0),
            _resident(bb.shape),
            _resident(cc.shape),
            _resident(a_re.shape),
            _resident(a_im.shape),
            _resident((1, D_MODEL)),
            _resident((D_MODEL, D_MODEL)),
            pl.BlockSpec((None,) + state_shape, lambda g, t: (0, 0, 0)),
        ],
        out_specs=[
            _rows(tt, D_MODEL),
            pl.BlockSpec((None,) + state_shape, lambda g, t: (g, 0, 0)),
        ],
        out_shape=[
            jax.ShapeDtypeStruct((ng, rows, D_MODEL), F32),
            jax.ShapeDtypeStruct((ng,) + state_shape, F32),
        ],
        scratch_shapes=[
            pltpu.VMEM(state_shape, F32),
            pltpu.VMEM((tt * SUBLANES, 2 * SSM_BLOCK_STATE), F32),
            pltpu.VMEM((tt * SUBLANES, D_MODEL), F32),
        ],
        compiler_params=_params(("parallel", "arbitrary")),
        name="s5",
    )(p, bb, cc, a_re, a_im, d_skip, w_glu, s0)


def _hgrn_kernel(q_ref, f_ref, i_ref, og_ref, lb_ref, ng_ref, s0_ref,
                 yb_ref, sfin_ref, st_ref, qin_ref, kout_ref, v_ref, oint_ref, *, tt, chunk, heads):
    ti = pl.program_id(2)

    @pl.when(ti == 0)
    def _():
        st_ref[...] = s0_ref[...]

    rows = chunk * SUBLANES
    r_idx = lax.broadcasted_iota(jnp.int32, (rows, rows), 0)
    c_idx = lax.broadcasted_iota(jnp.int32, (rows, rows), 1)
    allowed = (((r_idx - c_idx) & (SUBLANES - 1)) == 0) & (c_idx <= r_idx)
    mid = chunk // 2 - 1
    head = (chunk, SUBLANES, HGRN_HEAD_DIM)

    def head_chunk(hh, r):
        lanes = slice(hh * HGRN_HEAD_DIM, (hh + 1) * HGRN_HEAD_DIM)
        lb = lb_ref[:, lanes]
        f = lb + (1.0 - lb) * jax.nn.sigmoid(f_ref[r, lanes])
        k = (1.0 - f).reshape(head)
        cum = jnp.log(f).reshape(head)
        shift = 1
        while shift < chunk:
            cum = cum + jnp.concatenate([jnp.zeros((shift,) + head[1:], F32), cum[:-shift]], axis=0)
            shift *= 2
        ref_row = cum[mid:mid + 1]
        last = cum[chunk - 1:chunk]
        q = q_ref[r, lanes].reshape(head)
        q_mid = (q * jnp.exp(cum - ref_row)).reshape(rows, HGRN_HEAD_DIM).astype(BF16)
        k_mid = (k * jnp.exp(ref_row - cum)).reshape(rows, HGRN_HEAD_DIM).astype(BF16)
        qin_ref[hh] = (q * jnp.exp(cum)).reshape(rows, HGRN_HEAD_DIM)
        kout_ref[hh] = (k * jnp.exp(last - cum)).reshape(rows, HGRN_HEAD_DIM)
        v = i_ref[r, lanes]
        v_ref[hh] = v
        decay = jnp.exp(last[0])
        scores = lax.dot_general(q_mid, k_mid, (((1,), (1,)), ((), ())), preferred_element_type=F32)
        scores = jnp.where(allowed, scores, 0.0).astype(BF16)
        o = jnp.dot(scores, v.astype(BF16), preferred_element_type=F32)
        for b in range(SUBLANES):
            rb = pl.ds(b, chunk, stride=SUBLANES)
            state_t = st_ref[hh, b]
            oint_ref[hh, rb, :] = lax.dot_general(
                qin_ref[hh, rb, :].astype(BF16), state_t.astype(BF16), (((1,), (1,)), ((), ())),
                preferred_element_type=F32)
            st_ref[hh, b] = state_t * decay[b:b + 1, :] + lax.dot_general(
                v_ref[hh, rb, :].astype(BF16), kout_ref[hh, rb, :].astype(BF16), (((0,), (0,)), ((), ())),
                preferred_element_type=F32)
        o = _rmsnorm(o + oint_ref[hh], ng_ref[:, lanes])
        og = og_ref[r, lanes]
        yb_ref[r, lanes] = o * (og * jax.nn.sigmoid(og))

    def chunk_step(ci, carry):
        r = pl.ds(pl.multiple_of(ci * rows, rows), rows)
        for hh in range(heads):
            head_chunk(hh, r)
        return carry

    lax.fori_loop(0, tt // chunk, chunk_step, 0)

    @pl.when(ti == pl.num_programs(2) - 1)
    def _():
        sfin_ref[...] = st_ref[...]


def _hgrn(p, lb, ng_, s0, tt, chunk):
    ng, rows, _ = p.shape
    heads = HGRN_HEADS_PER_STEP
    width = heads * HGRN_HEAD_DIM
    steps = HGRN_HEADS // heads
    col = lambda k: pl.BlockSpec((None, tt * SUBLANES, width), lambda g, h, t, k=k: (g, t, k * steps + h))
    per_head = pl.BlockSpec((1, width), lambda g, h, t: (0, h))
    state_shape = (heads, SUBLANES, HGRN_HEAD_DIM, HGRN_HEAD_DIM)
    chunk_buf = pltpu.VMEM((heads, chunk * SUBLANES, HGRN_HEAD_DIM), F32)
    return pl.pallas_call(
        functools.partial(_hgrn_kernel, tt=tt, chunk=chunk, heads=heads),
        grid=(ng, steps, rows // (tt * SUBLANES)),
        in_specs=[col(1), col(2), col(3), col(4), per_head, per_head,
                  pl.BlockSpec((None,) + state_shape, lambda g, h, t: (0, h, 0, 0, 0))],
        out_specs=[
            pl.BlockSpec((None, tt * SUBLANES, width), lambda g, h, t: (g, t, h)),
            pl.BlockSpec((None,) + state_shape, lambda g, h, t: (g, h, 0, 0, 0)),
        ],
        out_shape=[
            jax.ShapeDtypeStruct((ng, rows, D_MODEL), F32),
            jax.ShapeDtypeStruct((ng, HGRN_HEADS, SUBLANES, HGRN_HEAD_DIM, HGRN_HEAD_DIM), F32),
        ],
        scratch_shapes=[pltpu.VMEM(state_shape, F32), chunk_buf, chunk_buf, chunk_buf, chunk_buf],
        compiler_params=_params(("parallel", "parallel", "arbitrary")),
        name="hgrn2",
    )(p, p, p, p, lb, ng_, s0)


def _merge_kernel(h_ref, ga_ref, gb_ref, ya_ref, yb_ref, wa_ref, wb_ref, wo_ref, o_ref):
    a = jnp.dot(ya_ref[...].astype(BF16), wa_ref[...], preferred_element_type=F32)
    b = jnp.dot(yb_ref[...].astype(BF16), wb_ref[...], preferred_element_type=F32)
    merged = jax.nn.sigmoid(ga_ref[...]) * a + jax.nn.sigmoid(gb_ref[...]) * b
    o_ref[...] = h_ref[...] + jnp.dot(merged.astype(BF16), wo_ref[...], preferred_element_type=F32)


def _merge(h, p, ya, yb, wa, wb, wo, tt):
    ng, rows, _ = h.shape
    w = _resident((D_MODEL, D_MODEL))
    return pl.pallas_call(
        _merge_kernel,
        grid=(ng, rows // (tt * SUBLANES)),
        in_specs=[_rows(tt, D_MODEL), _rows(tt, D_MODEL, 5), _rows(tt, D_MODEL, 6),
                  _rows(tt, D_MODEL), _rows(tt, D_MODEL), w, w, w],
        out_specs=_rows(tt, D_MODEL),
        out_shape=jax.ShapeDtypeStruct((ng, rows, D_MODEL), F32),
        compiler_params=_params(("parallel", "parallel")),
        name="merge",
    )(h, p, p, ya, yb, wa, wb, wo)


def _ffn_kernel(h_ref, g_ref, wg_ref, wv_ref, cwg_ref, cwv_ref, cbg_ref, cbv_ref, wd_ref, gf_ref,
                c0_ref, o_ref, cfin_ref, carry_ref, act_ref, slab_ref, *, tt):
    ti = pl.program_id(1)

    @pl.when(ti == 0)
    def _():
        carry_ref[...] = c0_ref[...]

    rows = tt * SUBLANES
    tail = (CONV_WIDTH - 1) * SUBLANES
    h = h_ref[...]
    z = _rmsnorm(h, g_ref[...]).astype(BF16)

    def conv(pre, carry_idx, cw, cb):
        ext = jnp.concatenate([carry_ref[carry_idx], pre], axis=0)
        carry_ref[carry_idx] = ext[rows:rows + tail]
        out = cb
        for j in range(CONV_WIDTH):
            out = out + ext[j * SUBLANES:j * SUBLANES + rows] * cw[j:j + 1]
        return out

    for n in range(N_FF_CHUNKS):
        gate = conv(jnp.dot(z, wg_ref[n], preferred_element_type=F32), n, cwg_ref[n], cbg_ref[n])
        val = conv(jnp.dot(z, wv_ref[n], preferred_element_type=F32), N_FF_CHUNKS + n,
                   cwv_ref[n], cbv_ref[n])
        act_ref[:, n * FF_CHUNK:(n + 1) * FF_CHUNK] = (gate * jax.nn.sigmoid(gate) * val).astype(BF16)

    down = jnp.dot(act_ref[...], wd_ref[...], preferred_element_type=F32)
    out = _rmsnorm(h + down, gf_ref[...])
    for j in range(N_LANE_SLABS):
        slab_ref[j] = out[:, j * LANES:(j + 1) * LANES]
    for b in range(SUBLANES):
        for j in range(N_LANE_SLABS):
            o_ref[b, :, j * LANES:(j + 1) * LANES] = slab_ref[j, pl.ds(b, tt, stride=SUBLANES), :]

    @pl.when(ti == pl.num_programs(1) - 1)
    def _():
        cfin_ref[...] = carry_ref[...]


def _ffn(h, g, wg, wv, cwg, cwv, cbg, cbv, wd, gf, c0, tt):
    ng, rows, _ = h.shape
    l = rows // SUBLANES
    carry_shape = (2 * N_FF_CHUNKS, (CONV_WIDTH - 1) * SUBLANES, FF_CHUNK)
    return pl.pallas_call(
        functools.partial(_ffn_kernel, tt=tt),
        grid=(ng, l // tt),
        in_specs=[
            _rows(tt, D_MODEL),
            _resident((1, D_MODEL)),
            _resident(wg.shape), _resident(wv.shape),
            _resident(cwg.shape), _resident(cwv.shape), _resident(cbg.shape), _resident(cbv.shape),
            _resident(wd.shape),
            _resident((1, D_MODEL)),
            pl.BlockSpec((None,) + carry_shape, lambda g, t: (0, 0, 0, 0)),
        ],
        out_specs=[
            pl.BlockSpec((SUBLANES, tt, D_MODEL), lambda g, t: (g, t, 0)),
            pl.BlockSpec((None,) + carry_shape, lambda g, t: (g, 0, 0, 0)),
        ],
        out_shape=[
            jax.ShapeDtypeStruct((ng * SUBLANES, l, D_MODEL), F32),
            jax.ShapeDtypeStruct((ng,) + carry_shape, F32),
        ],
        scratch_shapes=[
            pltpu.VMEM(carry_shape, F32),
            pltpu.VMEM((tt * SUBLANES, D_FF), BF16),
            pltpu.VMEM((N_LANE_SLABS, tt * SUBLANES, LANES), F32),
        ],
        compiler_params=_params(("parallel", "arbitrary")),
        name="ffn",
    )(h, g, wg, wv, cwg, cwv, cbg, cbv, wd, gf, c0)


def _s5_params(lam_re, lam_im, log_dt, b_re, b_im, c_re, c_im):
    dt = jnp.exp(log_dt)[:, None]
    mag = jnp.exp(lam_re * dt)
    ab_re = mag * jnp.cos(lam_im * dt)
    ab_im = mag * jnp.sin(lam_im * dt)
    den = lam_re * lam_re + lam_im * lam_im
    nr = ab_re - 1.0
    coef_re = (nr * lam_re + ab_im * lam_im) / den
    coef_im = (ab_im * lam_re - nr * lam_im) / den
    bb_re = coef_re[..., None] * b_re - coef_im[..., None] * b_im
    bb_im = coef_re[..., None] * b_im + coef_im[..., None] * b_re
    eye = jnp.eye(GROUPS_PER_BLOCK, dtype=F32)

    def pack_in(m):
        m = m.reshape(N_SSM_BLOCKS, GROUPS_PER_BLOCK, SSM_STATE, SSM_GROUP)
        return jnp.einsum("jgph,gk->jghkp", m, eye).reshape(N_SSM_BLOCKS, SSM_BLOCK_IN, SSM_BLOCK_STATE)

    def pack_out(m):
        m = m.reshape(N_SSM_BLOCKS, GROUPS_PER_BLOCK, SSM_GROUP, SSM_STATE)
        return jnp.einsum("jghp,gk->jgpkh", m, eye).reshape(N_SSM_BLOCKS, SSM_BLOCK_STATE, SSM_BLOCK_IN)

    bb = jnp.concatenate([pack_in(bb_re), pack_in(bb_im)], axis=2).astype(BF16)
    cc = jnp.concatenate([pack_out(c_re), -pack_out(c_im)], axis=1).astype(BF16)
    return bb, cc, ab_re.reshape(1, -1), ab_im.reshape(1, -1)


def _ff_chunks(a, axis):
    a = jnp.moveaxis(a, axis, -1)
    a = a.reshape(a.shape[:-1] + (N_FF_CHUNKS, FF_CHUNK))
    return jnp.moveaxis(a, -2, 0)


def _block(x, states, w):
    l = x.shape[1]
    s5_state, hgrn_state, conv_carry = states
    t_in, t_s5, t_hgrn, t_merge, t_ffn = (min(t, l) for t in TILES)
    p, h = _inproj(x, w["mix_g"], w["w_in"], t_in)
    ya, s5_state = _s5(p, w["bb"], w["cc"], w["a_re"], w["a_im"], w["d_skip"], w["w_glu"], s5_state, t_s5)
    yb, hgrn_state = _hgrn(p, w["lb"], w["hgrn_g"], hgrn_state, t_hgrn, min(HGRN_CHUNK, l))
    h1 = _merge(h, p, ya, yb, w["w_a"], w["w_b"], w["w_out"], t_merge)
    out, conv_carry = _ffn(h1, w["ffn_g"], w["w_up_g"], w["w_up_v"], w["cw_g"], w["cw_v"], w["cb_g"],
                           w["cb_v"], w["w_down"], w["final_g"], conv_carry, t_ffn)
    return out, (s5_state, hgrn_state, conv_carry)


def kernel(x, meta_tokens, mix_norm_g, w_in, ssm_lambda_re, ssm_lambda_im, ssm_log_dt, ssm_b_re, ssm_b_im, ssm_c_re, ssm_c_im, ssm_d, ssm_w_glu, w_ssm_proj, hgrn_lb_logits, hgrn_norm_g, w_hgrn_proj, w_out, ffn_norm_g, w_up, conv_w, conv_b, w_down, final_norm_g):
    assert x.shape[0] % SUBLANES == 0 and x.shape[2] == D_MODEL
    bb, cc, a_re, a_im = _s5_params(ssm_lambda_re[0], ssm_lambda_im[0], ssm_log_dt[0],
                                    ssm_b_re[0], ssm_b_im[0], ssm_c_re[0], ssm_c_im[0])
    lower_bounds = jnp.cumsum(jax.nn.softmax(hgrn_lb_logits.astype(F32), axis=0), axis=0)
    w = {
        "mix_g": mix_norm_g[0].reshape(1, D_MODEL),
        "w_in": w_in[0].astype(BF16),
        "bb": bb, "cc": cc, "a_re": a_re, "a_im": a_im,
        "d_skip": ssm_d[0].reshape(1, D_MODEL),
        "w_glu": ssm_w_glu[0].astype(BF16),
        "lb": lower_bounds[0].reshape(1, D_MODEL),
        "hgrn_g": jnp.tile(hgrn_norm_g[0], HGRN_HEADS).reshape(1, D_MODEL),
        "w_a": w_ssm_proj[0].astype(BF16),
        "w_b": w_hgrn_proj[0].astype(BF16),
        "w_out": w_out[0].astype(BF16),
        "ffn_g": ffn_norm_g[0].reshape(1, D_MODEL),
        "w_up_g": _ff_chunks(w_up[0][:, :D_FF], 1).astype(BF16),
        "w_up_v": _ff_chunks(w_up[0][:, D_FF:], 1).astype(BF16),
        "cw_g": _ff_chunks(conv_w[0][:, :D_FF], 1),
        "cw_v": _ff_chunks(conv_w[0][:, D_FF:], 1),
        "cb_g": _ff_chunks(conv_b[0][None, :D_FF], 1),
        "cb_v": _ff_chunks(conv_b[0][None, D_FF:], 1),
        "w_down": w_down[0].astype(BF16),
        "final_g": final_norm_g.reshape(1, D_MODEL),
    }
    zero_states = (
        jnp.zeros((1, SUBLANES, N_SSM_STATE), F32),
        jnp.zeros((1, HGRN_HEADS, SUBLANES, HGRN_HEAD_DIM, HGRN_HEAD_DIM), F32),
        jnp.zeros((1, 2 * N_FF_CHUNKS, (CONV_WIDTH - 1) * SUBLANES, FF_CHUNK), F32),
    )
    meta = jnp.broadcast_to(meta_tokens.astype(x.dtype)[None], (SUBLANES, N_META, D_MODEL))
    _, states = _block(meta, zero_states, w)
    out, _ = _block(x, states, w)
    return out
```

```python
import functools

import numpy as np
import jax
import jax.numpy as jnp
from jax import lax
from jax.experimental import pallas as pl
from jax.experimental.pallas import tpu as pltpu

F32 = jnp.float32
BF16 = jnp.bfloat16

D_MODEL = 1024
N_META = 16
SSM_GROUPS = 64
SSM_GROUP = 16
SSM_STATE = 64
SSM_CHUNK = 16
SSM_FLAT = SSM_CHUNK * SSM_GROUP
SSM_STATE_LANES = 2 * SSM_STATE
HGRN_HEADS = 8
HGRN_HEAD_DIM = 128
HGRN_CHUNK = 32
HGRN_HEADS_PER_STEP = 2
D_FF = 2816
FF_CHUNK = 256
N_FF_CHUNKS = D_FF // FF_CHUNK
CONV_WIDTH = 3
IN_BLOCKS = 7
EPS = 1e-6
LANES = 128
SUBLANES = 8
N_LANE_SLABS = D_MODEL // LANES
GROUPS_PER_SLAB = LANES // SSM_GROUP
VMEM_LIMIT_BYTES = 56 * 1024 * 1024
TILES = (32, 128, 256, 64, 64)


def _rmsnorm(x, g):
    return x * lax.rsqrt(jnp.mean(x * x, axis=-1, keepdims=True) + EPS) * g


def _resident(shape):
    n = len(shape)
    return pl.BlockSpec(shape, lambda *_: (0,) * n, pipeline_mode=pl.Buffered(1))


def _params(semantics):
    return pltpu.CompilerParams(dimension_semantics=semantics, vmem_limit_bytes=VMEM_LIMIT_BYTES)


def _rows(tt, cols, k=0):
    return pl.BlockSpec((None, tt * SUBLANES, cols), lambda g, t, k=k: (g, t, k))


def _inproj_kernel(x_ref, g_ref, w_ref, p_ref, h_ref, slab_ref, *, tt):
    for b in range(SUBLANES):
        for j in range(N_LANE_SLABS):
            slab_ref[j, pl.ds(b, tt, stride=SUBLANES), :] = x_ref[b, :, j * LANES:(j + 1) * LANES]
    h = jnp.concatenate([slab_ref[j] for j in range(N_LANE_SLABS)], axis=1)
    h_ref[...] = h
    z = _rmsnorm(h, g_ref[...]).astype(BF16)
    for k in range(IN_BLOCKS):
        cols = slice(k * D_MODEL, (k + 1) * D_MODEL)
        p_ref[:, cols] = jnp.dot(z, w_ref[:, cols], preferred_element_type=F32).astype(BF16)


def _inproj(x, g, w, tt):
    b, l, _ = x.shape
    ng = b // SUBLANES
    return pl.pallas_call(
        functools.partial(_inproj_kernel, tt=tt),
        grid=(ng, l // tt),
        in_specs=[
            pl.BlockSpec((SUBLANES, tt, D_MODEL), lambda i, t: (i, t, 0)),
            _resident((1, D_MODEL)),
            _resident((D_MODEL, IN_BLOCKS * D_MODEL)),
        ],
        out_specs=[_rows(tt, IN_BLOCKS * D_MODEL), _rows(tt, D_MODEL)],
        out_shape=[
            jax.ShapeDtypeStruct((ng, l * SUBLANES, IN_BLOCKS * D_MODEL), BF16),
            jax.ShapeDtypeStruct((ng, l * SUBLANES, D_MODEL), F32),
        ],
        scratch_shapes=[pltpu.VMEM((N_LANE_SLABS, tt * SUBLANES, LANES), F32)],
        compiler_params=_params(("parallel", "parallel")),
        name="inproj",
    )(x, g, w)


def _s5_kernel(u_ref, w1_ref, ms_ref, c1_ref, c2_ref, d_ref, wglu_ref, s0_ref,
               ya_ref, sfin_ref, st_ref, z_ref, o_ref, y_ref, *, tt):
    ti = pl.program_id(1)

    @pl.when(ti == 0)
    def _():
        st_ref[...] = s0_ref[...]

    nc = tt // SSM_CHUNK
    m = nc * SUBLANES
    chunk_rows = SSM_CHUNK * SUBLANES
    half = SSM_CHUNK // 2
    lane_slot = lax.broadcasted_iota(jnp.int32, (m, LANES), 1) // SSM_GROUP
    in_slot = [lane_slot == s for s in range(GROUPS_PER_SLAB)]

    u = u_ref[...].astype(F32)

    def step_rows(x, t, j):
        return jnp.concatenate(
            [x[c * chunk_rows + t * SUBLANES:c * chunk_rows + (t + 1) * SUBLANES, j * LANES:(j + 1) * LANES]
             for c in range(nc)], axis=0)

    def slab_outputs(j):
        groups = range(j * GROUPS_PER_SLAB, (j + 1) * GROUPS_PER_SLAB)
        gs = slice(j * GROUPS_PER_SLAB, (j + 1) * GROUPS_PER_SLAB)
        rs = [jnp.dot(z_ref[g], w1_ref[g], preferred_element_type=F32) for g in groups]
        s, sw = st_ref[0, gs], st_ref[1, gs]
        c1, c2 = c1_ref[gs], c2_ref[gs]
        carried = []
        for c in range(nc):
            carried.append(s)
            rows_c = slice(c * SUBLANES, (c + 1) * SUBLANES)
            inc = jnp.stack([r[rows_c, SSM_FLAT:SSM_FLAT + SSM_STATE_LANES] for r in rs])
            inc_sw = jnp.stack([r[rows_c, SSM_FLAT + SSM_STATE_LANES:] for r in rs])
            s, sw = c1 * s + c2 * sw + inc, c1 * sw - c2 * s + inc_sw
        st_ref[0, gs] = s
        st_ref[1, gs] = sw
        for i, g in enumerate(groups):
            before = jnp.concatenate([cs[i] for cs in carried], axis=0).astype(BF16)
            o_ref[g] = rs[i][:, :SSM_FLAT] + jnp.dot(before, ms_ref[g], preferred_element_type=F32)

    for j in range(N_LANE_SLABS):
        rot = []
        for t in range(SSM_CHUNK):
            k = t % half
            x = step_rows(u, t, j)
            rot.append(x if k == 0 else pltpu.roll(x, SSM_GROUP * k, axis=1))
        for gg in range(GROUPS_PER_SLAB):
            for hf in range(2):
                acc = rot[half * hf]
                for k in range(1, half):
                    acc = jnp.where(in_slot[(gg + k) % half], rot[half * hf + k], acc)
                z_ref[j * GROUPS_PER_SLAB + gg, :, hf * LANES:(hf + 1) * LANES] = acc.astype(BF16)

    for j in range(N_LANE_SLABS):
        slab_outputs(j)

    for j in range(N_LANE_SLABS):
        for hf in range(2):
            ys = [o_ref[j * GROUPS_PER_SLAB + gg, :, hf * LANES:(hf + 1) * LANES]
                  for gg in range(GROUPS_PER_SLAB)]
            for k in range(half):
                acc = ys[0]
                for gg in range(1, GROUPS_PER_SLAB):
                    acc = jnp.where(in_slot[(gg + k) % half], ys[gg], acc)
                nat = acc if k == 0 else pltpu.roll(acc, LANES - SSM_GROUP * k, axis=1)
                t = half * hf + k
                for c in range(nc):
                    y_ref[c * chunk_rows + t * SUBLANES:c * chunk_rows + (t + 1) * SUBLANES,
                          j * LANES:(j + 1) * LANES] = nat[c * SUBLANES:(c + 1) * SUBLANES]

    y = jax.nn.gelu(y_ref[...] + d_ref[...] * u)
    gate = jnp.dot(y.astype(BF16), wglu_ref[...], preferred_element_type=F32)
    ya_ref[...] = (y * jax.nn.sigmoid(gate)).astype(BF16)

    @pl.when(ti == pl.num_programs(1) - 1)
    def _():
        sfin_ref[...] = st_ref[...]


def _s5(p, w1, ms, c1, c2, d_skip, w_glu, s0, tt):
    ng, rows, _ = p.shape
    m = tt // SSM_CHUNK * SUBLANES
    state_shape = (2, SSM_GROUPS, SUBLANES, SSM_STATE_LANES)
    return pl.pallas_call(
        functools.partial(_s5_kernel, tt=tt),
        grid=(ng, rows // (tt * SUBLANES)),
        in_specs=[
            _rows(tt, D_MODEL, 0),
            _resident(w1.shape),
            _resident(ms.shape),
            _resident(c1.shape),
            _resident(c2.shape),
            _resident((1, D_MODEL)),
            _resident((D_MODEL, D_MODEL)),
            pl.BlockSpec((None,) + state_shape, lambda g, t: (0, 0, 0, 0, 0)),
        ],
        out_specs=[
            _rows(tt, D_MODEL),
            pl.BlockSpec((None,) + state_shape, lambda g, t: (g, 0, 0, 0, 0)),
        ],
        out_shape=[
            jax.ShapeDtypeStruct((ng, rows, D_MODEL), BF16),
            jax.ShapeDtypeStruct((ng,) + state_shape, F32),
        ],
        scratch_shapes=[
            pltpu.VMEM(state_shape, F32),
            pltpu.VMEM((SSM_GROUPS, m, SSM_FLAT), BF16),
            pltpu.VMEM((SSM_GROUPS, m, SSM_FLAT), F32),
            pltpu.VMEM((tt * SUBLANES, D_MODEL), F32),
        ],
        compiler_params=_params(("parallel", "arbitrary")),
        name="s5",
    )(p, w1, ms, c1, c2, d_skip, w_glu, s0)


def _hgrn_kernel(q_ref, f_ref, i_ref, og_ref, lb_ref, ng_ref, s0_ref,
                 yb_ref, sfin_ref, st_ref, qin_ref, kout_ref, v_ref, oint_ref, *, tt, chunk, heads):
    ti = pl.program_id(2)

    @pl.when(ti == 0)
    def _():
        st_ref[...] = s0_ref[...]

    rows = chunk * SUBLANES
    r_idx = lax.broadcasted_iota(jnp.int32, (rows, rows), 0)
    c_idx = lax.broadcasted_iota(jnp.int32, (rows, rows), 1)
    allowed = (((r_idx - c_idx) & (SUBLANES - 1)) == 0) & (c_idx <= r_idx)
    mid = chunk // 2 - 1
    head = (chunk, SUBLANES, HGRN_HEAD_DIM)

    def head_chunk(hh, r):
        lanes = slice(hh * HGRN_HEAD_DIM, (hh + 1) * HGRN_HEAD_DIM)
        lb = lb_ref[:, lanes]
        f = lb + (1.0 - lb) * jax.nn.sigmoid(f_ref[r, lanes].astype(F32))
        k = (1.0 - f).reshape(head)
        cum = jnp.log(f).reshape(head)
        shift = 1
        while shift < chunk:
            cum = cum + jnp.concatenate([jnp.zeros((shift,) + head[1:], F32), cum[:-shift]], axis=0)
            shift *= 2
        ref_row = cum[mid:mid + 1]
        last = cum[chunk - 1:chunk]
        q = q_ref[r, lanes].astype(F32).reshape(head)
        q_mid = (q * jnp.exp(cum - ref_row)).reshape(rows, HGRN_HEAD_DIM).astype(BF16)
        k_mid = (k * jnp.exp(ref_row - cum)).reshape(rows, HGRN_HEAD_DIM).astype(BF16)
        qin_ref[hh] = (q * jnp.exp(cum)).reshape(rows, HGRN_HEAD_DIM)
        kout_ref[hh] = (k * jnp.exp(last - cum)).reshape(rows, HGRN_HEAD_DIM)
        v = i_ref[r, lanes]
        v_ref[hh] = v.astype(F32)
        decay = jnp.exp(last[0])
        scores = lax.dot_general(q_mid, k_mid, (((1,), (1,)), ((), ())), preferred_element_type=F32)
        scores = jnp.where(allowed, scores, 0.0).astype(BF16)
        o = jnp.dot(scores, v, preferred_element_type=F32)
        for b in range(SUBLANES):
            rb = pl.ds(b, chunk, stride=SUBLANES)
            state_t = st_ref[hh, b]
            oint_ref[hh, rb, :] = lax.dot_general(
                qin_ref[hh, rb, :].astype(BF16), state_t.astype(BF16), (((1,), (1,)), ((), ())),
                preferred_element_type=F32)
            st_ref[hh, b] = state_t * decay[b:b + 1, :] + lax.dot_general(
                v_ref[hh, rb, :].astype(BF16), kout_ref[hh, rb, :].astype(BF16), (((0,), (0,)), ((), ())),
                preferred_element_type=F32)
        o = _rmsnorm(o + oint_ref[hh], ng_ref[:, lanes])
        og = og_ref[r, lanes].astype(F32)
        yb_ref[r, lanes] = (o * (og * jax.nn.sigmoid(og))).astype(BF16)

    def chunk_step(ci, carry):
        r = pl.ds(pl.multiple_of(ci * rows, rows), rows)
        for hh in range(heads):
            head_chunk(hh, r)
        return carry

    lax.fori_loop(0, tt // chunk, chunk_step, 0)

    @pl.when(ti == pl.num_programs(2) - 1)
    def _():
        sfin_ref[...] = st_ref[...]


def _hgrn(p, lb, ng_, s0, tt, chunk):
    ng, rows, _ = p.shape
    heads = HGRN_HEADS_PER_STEP
    width = heads * HGRN_HEAD_DIM
    steps = HGRN_HEADS // heads
    col = lambda k: pl.BlockSpec((None, tt * SUBLANES, width), lambda g, h, t, k=k: (g, t, k * steps + h))
    per_head = pl.BlockSpec((1, width), lambda g, h, t: (0, h))
    state_shape = (heads, SUBLANES, HGRN_HEAD_DIM, HGRN_HEAD_DIM)
    chunk_buf = pltpu.VMEM((heads, chunk * SUBLANES, HGRN_HEAD_DIM), F32)
    return pl.pallas_call(
        functools.partial(_hgrn_kernel, tt=tt, chunk=chunk, heads=heads),
        grid=(ng, steps, rows // (tt * SUBLANES)),
        in_specs=[col(1), col(2), col(3), col(4), per_head, per_head,
                  pl.BlockSpec((None,) + state_shape, lambda g, h, t: (0, h, 0, 0, 0))],
        out_specs=[
            pl.BlockSpec((None, tt * SUBLANES, width), lambda g, h, t: (g, t, h)),
            pl.BlockSpec((None,) + state_shape, lambda g, h, t: (g, h, 0, 0, 0)),
        ],
        out_shape=[
            jax.ShapeDtypeStruct((ng, rows, D_MODEL), BF16),
            jax.ShapeDtypeStruct((ng, HGRN_HEADS, SUBLANES, HGRN_HEAD_DIM, HGRN_HEAD_DIM), F32),
        ],
        scratch_shapes=[pltpu.VMEM(state_shape, F32), chunk_buf, chunk_buf, chunk_buf, chunk_buf],
        compiler_params=_params(("parallel", "parallel", "arbitrary")),
        name="hgrn2",
    )(p, p, p, p, lb, ng_, s0)


def _merge_kernel(h_ref, ga_ref, gb_ref, ya_ref, yb_ref, wa_ref, wb_ref, wo_ref, o_ref):
    a = jnp.dot(ya_ref[...], wa_ref[...], preferred_element_type=F32)
    b = jnp.dot(yb_ref[...], wb_ref[...], preferred_element_type=F32)
    merged = (jax.nn.sigmoid(ga_ref[...].astype(F32)) * a
              + jax.nn.sigmoid(gb_ref[...].astype(F32)) * b)
    o_ref[...] = h_ref[...] + jnp.dot(merged.astype(BF16), wo_ref[...], preferred_element_type=F32)


def _merge(h, p, ya, yb, wa, wb, wo, tt):
    ng, rows, _ = h.shape
    w = _resident((D_MODEL, D_MODEL))
    return pl.pallas_call(
        _merge_kernel,
        grid=(ng, rows // (tt * SUBLANES)),
        in_specs=[_rows(tt, D_MODEL), _rows(tt, D_MODEL, 5), _rows(tt, D_MODEL, 6),
                  _rows(tt, D_MODEL), _rows(tt, D_MODEL), w, w, w],
        out_specs=_rows(tt, D_MODEL),
        out_shape=jax.ShapeDtypeStruct((ng, rows, D_MODEL), F32),
        compiler_params=_params(("parallel", "parallel")),
        name="merge",
    )(h, p, p, ya, yb, wa, wb, wo)


def _ffn_kernel(h_ref, g_ref, wg_ref, wv_ref, cwg_ref, cwv_ref, cbg_ref, cbv_ref, wd_ref, gf_ref,
                c0_ref, o_ref, cfin_ref, carry_ref, act_ref, slab_ref, *, tt):
    ti = pl.program_id(1)

    @pl.when(ti == 0)
    def _():
        carry_ref[...] = c0_ref[...]

    rows = tt * SUBLANES
    tail = (CONV_WIDTH - 1) * SUBLANES
    h = h_ref[...]
    z = _rmsnorm(h, g_ref[...]).astype(BF16)

    def conv(pre, carry_idx, cw, cb):
        ext = jnp.concatenate([carry_ref[carry_idx], pre], axis=0)
        carry_ref[carry_idx] = ext[rows:rows + tail]
        out = cb
        for j in range(CONV_WIDTH):
            out = out + ext[j * SUBLANES:j * SUBLANES + rows] * cw[j:j + 1]
        return out

    for n in range(N_FF_CHUNKS):
        gate = conv(jnp.dot(z, wg_ref[n], preferred_element_type=F32), n, cwg_ref[n], cbg_ref[n])
        val = conv(jnp.dot(z, wv_ref[n], preferred_element_type=F32), N_FF_CHUNKS + n,
                   cwv_ref[n], cbv_ref[n])
        act_ref[:, n * FF_CHUNK:(n + 1) * FF_CHUNK] = (gate * jax.nn.sigmoid(gate) * val).astype(BF16)

    down = jnp.dot(act_ref[...], wd_ref[...], preferred_element_type=F32)
    out = _rmsnorm(h + down, gf_ref[...])
    for j in range(N_LANE_SLABS):
        slab_ref[j] = out[:, j * LANES:(j + 1) * LANES]
    for b in range(SUBLANES):
        for j in range(N_LANE_SLABS):
            o_ref[b, :, j * LANES:(j + 1) * LANES] = slab_ref[j, pl.ds(b, tt, stride=SUBLANES), :]

    @pl.when(ti == pl.num_programs(1) - 1)
    def _():
        cfin_ref[...] = carry_ref[...]


def _ffn(h, g, wg, wv, cwg, cwv, cbg, cbv, wd, gf, c0, tt):
    ng, rows, _ = h.shape
    l = rows // SUBLANES
    carry_shape = (2 * N_FF_CHUNKS, (CONV_WIDTH - 1) * SUBLANES, FF_CHUNK)
    return pl.pallas_call(
        functools.partial(_ffn_kernel, tt=tt),
        grid=(ng, l // tt),
        in_specs=[
            _rows(tt, D_MODEL),
            _resident((1, D_MODEL)),
            _resident(wg.shape), _resident(wv.shape),
            _resident(cwg.shape), _resident(cwv.shape), _resident(cbg.shape), _resident(cbv.shape),
            _resident(wd.shape),
            _resident((1, D_MODEL)),
            pl.BlockSpec((None,) + carry_shape, lambda g, t: (0, 0, 0, 0)),
        ],
        out_specs=[
            pl.BlockSpec((SUBLANES, tt, D_MODEL), lambda g, t: (g, t, 0)),
            pl.BlockSpec((None,) + carry_shape, lambda g, t: (g, 0, 0, 0)),
        ],
        out_shape=[
            jax.ShapeDtypeStruct((ng * SUBLANES, l, D_MODEL), F32),
            jax.ShapeDtypeStruct((ng,) + carry_shape, F32),
        ],
        scratch_shapes=[
            pltpu.VMEM(carry_shape, F32),
            pltpu.VMEM((tt * SUBLANES, D_FF), BF16),
            pltpu.VMEM((N_LANE_SLABS, tt * SUBLANES, LANES), F32),
        ],
        compiler_params=_params(("parallel", "arbitrary")),
        name="ffn",
    )(h, g, wg, wv, cwg, cwv, cbg, cbv, wd, gf, c0)


def _s5_params(lam_re, lam_im, log_dt, b_re, b_im, c_re, c_im):
    hp = lax.Precision.HIGHEST
    t_len = SSM_CHUNK
    dt = jnp.exp(log_dt)[:, None]
    mag = jnp.exp(lam_re * dt)
    ab_re = mag * jnp.cos(lam_im * dt)
    ab_im = mag * jnp.sin(lam_im * dt)
    den = lam_re * lam_re + lam_im * lam_im
    nr = ab_re - 1.0
    coef_re = (nr * lam_re + ab_im * lam_im) / den
    coef_im = (ab_im * lam_re - nr * lam_im) / den
    bb_re = coef_re[..., None] * b_re - coef_im[..., None] * b_im
    bb_im = coef_re[..., None] * b_im + coef_im[..., None] * b_re
    tau = jnp.arange(t_len + 1, dtype=F32)[None, :, None]
    pw_mag = jnp.exp((lam_re * dt)[:, None, :] * tau)
    pw_re = pw_mag * jnp.cos((lam_im * dt)[:, None, :] * tau)
    pw_im = pw_mag * jnp.sin((lam_im * dt)[:, None, :] * tau)
    ab_b_re = pw_re[..., None] * bb_re[:, None] - pw_im[..., None] * bb_im[:, None]
    ab_b_im = pw_re[..., None] * bb_im[:, None] + pw_im[..., None] * bb_re[:, None]
    kern = (jnp.einsum("gop,gtpi->gtoi", c_re, ab_b_re[:, :t_len], precision=hp)
            - jnp.einsum("gop,gtpi->gtoi", c_im, ab_b_im[:, :t_len], precision=hp))
    lag = np.arange(t_len)[None, :] - np.arange(t_len)[:, None]
    toep = jnp.where((lag >= 0)[None, :, :, None, None], kern[:, np.clip(lag, 0, None)], 0.0)
    toep = toep.transpose(0, 1, 4, 2, 3)
    inc_re = ab_b_re[:, t_len - 1 - np.arange(t_len)].transpose(0, 1, 3, 2)
    inc_im = ab_b_im[:, t_len - 1 - np.arange(t_len)].transpose(0, 1, 3, 2)
    inc = jnp.concatenate([inc_re, inc_im], axis=-1)
    inc_sw = jnp.concatenate([inc_im, inc_re], axis=-1)
    out_re = (c_re[:, None] * pw_re[:, 1:, None, :] - c_im[:, None] * pw_im[:, 1:, None, :])
    out_im = -(c_re[:, None] * pw_im[:, 1:, None, :] + c_im[:, None] * pw_re[:, 1:, None, :])
    ms = jnp.concatenate([out_re, out_im], axis=-1).transpose(0, 3, 1, 2)
    half = t_len // 2
    slot = np.arange(t_len)
    step_of = np.stack([half * (slot // half) + (slot % half - gg) % half
                        for gg in range(GROUPS_PER_SLAB)])
    order = jnp.asarray(np.tile(step_of, (SSM_GROUPS // GROUPS_PER_SLAB, 1)))
    take_t = lambda a, axis: jnp.take_along_axis(
        a, order.reshape((SSM_GROUPS,) + (1,) * (axis - 1) + (t_len,) + (1,) * (a.ndim - axis - 1)), axis=axis)
    toep = take_t(take_t(toep, 1), 3).reshape(SSM_GROUPS, SSM_FLAT, SSM_FLAT)
    inc = take_t(inc, 1).reshape(SSM_GROUPS, SSM_FLAT, SSM_STATE_LANES)
    inc_sw = take_t(inc_sw, 1).reshape(SSM_GROUPS, SSM_FLAT, SSM_STATE_LANES)
    ms = take_t(ms, 2).reshape(SSM_GROUPS, SSM_STATE_LANES, SSM_FLAT)
    w1 = jnp.concatenate([toep, inc, inc_sw], axis=-1).astype(BF16)
    a16_re, a16_im = pw_re[:, t_len], pw_im[:, t_len]
    c1 = jnp.concatenate([a16_re, a16_re], axis=-1)
    c2 = jnp.concatenate([-a16_im, a16_im], axis=-1)
    bcast = lambda a: jnp.broadcast_to(a[:, None, :], (SSM_GROUPS, SUBLANES, SSM_STATE_LANES))
    return w1, ms.astype(BF16), bcast(c1), bcast(c2)


def _ff_chunks(a, axis):
    a = jnp.moveaxis(a, axis, -1)
    a = a.reshape(a.shape[:-1] + (N_FF_CHUNKS, FF_CHUNK))
    return jnp.moveaxis(a, -2, 0)


def _block(x, states, w):
    l = x.shape[1]
    s5_state, hgrn_state, conv_carry = states
    t_in, t_s5, t_hgrn, t_merge, t_ffn = (min(t, l) for t in TILES)
    p, h = _inproj(x, w["mix_g"], w["w_in"], t_in)
    ya, s5_state = _s5(p, w["s5_w1"], w["s5_ms"], w["s5_c1"], w["s5_c2"], w["d_skip"], w["w_glu"],
                       s5_state, t_s5)
    yb, hgrn_state = _hgrn(p, w["lb"], w["hgrn_g"], hgrn_state, t_hgrn, min(HGRN_CHUNK, l))
    h1 = _merge(h, p, ya, yb, w["w_a"], w["w_b"], w["w_out"], t_merge)
    out, conv_carry = _ffn(h1, w["ffn_g"], w["w_up_g"], w["w_up_v"], w["cw_g"], w["cw_v"], w["cb_g"],
                           w["cb_v"], w["w_down"], w["final_g"], conv_carry, t_ffn)
    return out, (s5_state, hgrn_state, conv_carry)


def kernel(x, meta_tokens, mix_norm_g, w_in, ssm_lambda_re, ssm_lambda_im, ssm_log_dt, ssm_b_re, ssm_b_im, ssm_c_re, ssm_c_im, ssm_d, ssm_w_glu, w_ssm_proj, hgrn_lb_logits, hgrn_norm_g, w_hgrn_proj, w_out, ffn_norm_g, w_up, conv_w, conv_b, w_down, final_norm_g):
    assert x.shape[0] % SUBLANES == 0 and x.shape[1] % SSM_CHUNK == 0 and x.shape[2] == D_MODEL
    s5_w1, s5_ms, s5_c1, s5_c2 = _s5_params(ssm_lambda_re[0], ssm_lambda_im[0], ssm_log_dt[0],
                                            ssm_b_re[0], ssm_b_im[0], ssm_c_re[0], ssm_c_im[0])
    lower_bounds = jnp.cumsum(jax.nn.softmax(hgrn_lb_logits.astype(F32), axis=0), axis=0)
    w = {
        "mix_g": mix_norm_g[0].reshape(1, D_MODEL),
        "w_in": w_in[0].astype(BF16),
        "s5_w1": s5_w1, "s5_ms": s5_ms, "s5_c1": s5_c1, "s5_c2": s5_c2,
        "d_skip": ssm_d[0].reshape(1, D_MODEL),
        "w_glu": ssm_w_glu[0].astype(BF16),
        "lb": lower_bounds[0].reshape(1, D_MODEL),
        "hgrn_g": jnp.tile(hgrn_norm_g[0], HGRN_HEADS).reshape(1, D_MODEL),
        "w_a": w_ssm_proj[0].astype(BF16),
        "w_b": w_hgrn_proj[0].astype(BF16),
        "w_out": w_out[0].astype(BF16),
        "ffn_g": ffn_norm_g[0].reshape(1, D_MODEL),
        "w_up_g": _ff_chunks(w_up[0][:, :D_FF], 1).astype(BF16),
        "w_up_v": _ff_chunks(w_up[0][:, D_FF:], 1).astype(BF16),
        "cw_g": _ff_chunks(conv_w[0][:, :D_FF], 1),
        "cw_v": _ff_chunks(conv_w[0][:, D_FF:], 1),
        "cb_g": _ff_chunks(conv_b[0][None, :D_FF], 1),
        "cb_v": _ff_chunks(conv_b[0][None, D_FF:], 1),
        "w_down": w_down[0].astype(BF16),
        "final_g": final_norm_g.reshape(1, D_MODEL),
    }
    zero_states = (
        jnp.zeros((1, 2, SSM_GROUPS, SUBLANES, SSM_STATE_LANES), F32),
        jnp.zeros((1, HGRN_HEADS, SUBLANES, HGRN_HEAD_DIM, HGRN_HEAD_DIM), F32),
        jnp.zeros((1, 2 * N_FF_CHUNKS, (CONV_WIDTH - 1) * SUBLANES, FF_CHUNK), F32),
    )
    meta = jnp.broadcast_to(meta_tokens.astype(x.dtype)[None], (SUBLANES, N_META, D_MODEL))
    _, states = _block(meta, zero_states, w)
    out, _ = _block(x, states, w)
    return out
```

```python
import functools
import math

import numpy as np
import jax
import jax.numpy as jnp
from jax import lax
from jax.experimental import pallas as pl
from jax.experimental.pallas import tpu as pltpu

F32 = jnp.float32
BF16 = jnp.bfloat16

D_MODEL = 1024
N_META = 16
SSM_GROUPS = 64
SSM_GROUP = 16
SSM_STATE = 64
SSM_CHUNK = 16
SSM_FLAT = SSM_CHUNK * SSM_GROUP
SSM_STATE_LANES = 2 * SSM_STATE
HGRN_HEADS = 8
HGRN_HEAD_DIM = 128
HGRN_CHUNK = 32
HGRN_HEADS_PER_STEP = 2
D_FF = 2816
FF_CHUNK = 256
N_FF_CHUNKS = D_FF // FF_CHUNK
CONV_WIDTH = 3
IN_BLOCKS = 7
EPS = 1e-6
GELU_C0 = math.sqrt(2.0 / math.pi)
GELU_C1 = 0.044715 * GELU_C0
LANES = 128
SUBLANES = 8
N_LANE_SLABS = D_MODEL // LANES
GROUPS_PER_SLAB = LANES // SSM_GROUP
VMEM_LIMIT_BYTES = 56 * 1024 * 1024
TILES = (32, 128, 256, 64, 64)


def _rmsnorm(x, g):
    return x * lax.rsqrt(jnp.mean(x * x, axis=-1, keepdims=True) + EPS) * g


def _sigmoid(x):
    return 0.5 * jnp.tanh(0.5 * x) + 0.5


def _silu(x):
    h = 0.5 * x
    return h + h * jnp.tanh(h)


def _gelu_tanh(x):
    h = 0.5 * x
    return h + h * jnp.tanh(x * (GELU_C0 + GELU_C1 * (x * x)))


def _resident(shape):
    n = len(shape)
    return pl.BlockSpec(shape, lambda *_: (0,) * n, pipeline_mode=pl.Buffered(1))


def _params(semantics):
    return pltpu.CompilerParams(dimension_semantics=semantics, vmem_limit_bytes=VMEM_LIMIT_BYTES)


def _rows(tt, cols, k=0):
    return pl.BlockSpec((None, tt * SUBLANES, cols), lambda g, t, k=k: (g, t, k))


def _inproj_kernel(x_ref, g_ref, w_ref, p_ref, h_ref, slab_ref, *, tt):
    for b in range(SUBLANES):
        for j in range(N_LANE_SLABS):
            slab_ref[j, pl.ds(b, tt, stride=SUBLANES), :] = x_ref[b, :, j * LANES:(j + 1) * LANES]
    h = jnp.concatenate([slab_ref[j] for j in range(N_LANE_SLABS)], axis=1)
    h_ref[...] = h
    z = _rmsnorm(h, g_ref[...]).astype(BF16)
    for k in range(IN_BLOCKS):
        cols = slice(k * D_MODEL, (k + 1) * D_MODEL)
        p_ref[:, cols] = jnp.dot(z, w_ref[:, cols], preferred_element_type=F32).astype(BF16)


def _inproj(x, g, w, tt):
    b, l, _ = x.shape
    ng = b // SUBLANES
    return pl.pallas_call(
        functools.partial(_inproj_kernel, tt=tt),
        grid=(ng, l // tt),
        in_specs=[
            pl.BlockSpec((SUBLANES, tt, D_MODEL), lambda i, t: (i, t, 0)),
            _resident((1, D_MODEL)),
            _resident((D_MODEL, IN_BLOCKS * D_MODEL)),
        ],
        out_specs=[_rows(tt, IN_BLOCKS * D_MODEL), _rows(tt, D_MODEL)],
        out_shape=[
            jax.ShapeDtypeStruct((ng, l * SUBLANES, IN_BLOCKS * D_MODEL), BF16),
            jax.ShapeDtypeStruct((ng, l * SUBLANES, D_MODEL), F32),
        ],
        scratch_shapes=[pltpu.VMEM((N_LANE_SLABS, tt * SUBLANES, LANES), F32)],
        compiler_params=_params(("parallel", "parallel")),
        name="inproj",
    )(x, g, w)


def _s5_kernel(u_ref, w1_ref, ms_ref, c1_ref, c2_ref, d_ref, wglu_ref, s0_ref,
               ya_ref, sfin_ref, st_ref, z_ref, o_ref, y_ref, *, tt):
    ti = pl.program_id(1)

    @pl.when(ti == 0)
    def _():
        st_ref[...] = s0_ref[...]

    nc = tt // SSM_CHUNK
    m = nc * SUBLANES
    chunk_rows = SSM_CHUNK * SUBLANES
    half = SSM_CHUNK // 2
    lane_slot = lax.broadcasted_iota(jnp.int32, (m, LANES), 1) // SSM_GROUP
    in_slot = [lane_slot == s for s in range(GROUPS_PER_SLAB)]

    u = u_ref[...].astype(F32)

    def step_rows(x, t, j):
        return jnp.concatenate(
            [x[c * chunk_rows + t * SUBLANES:c * chunk_rows + (t + 1) * SUBLANES, j * LANES:(j + 1) * LANES]
             for c in range(nc)], axis=0)

    def slab_outputs(j):
        groups = range(j * GROUPS_PER_SLAB, (j + 1) * GROUPS_PER_SLAB)
        gs = slice(j * GROUPS_PER_SLAB, (j + 1) * GROUPS_PER_SLAB)
        rs = [jnp.dot(z_ref[g], w1_ref[g], preferred_element_type=F32) for g in groups]
        s, sw = st_ref[0, gs], st_ref[1, gs]
        c1, c2 = c1_ref[gs], c2_ref[gs]
        carried = []
        for c in range(nc):
            carried.append(s)
            rows_c = slice(c * SUBLANES, (c + 1) * SUBLANES)
            inc = jnp.stack([r[rows_c, SSM_FLAT:SSM_FLAT + SSM_STATE_LANES] for r in rs])
            inc_sw = jnp.stack([r[rows_c, SSM_FLAT + SSM_STATE_LANES:] for r in rs])
            s, sw = c1 * s + c2 * sw + inc, c1 * sw - c2 * s + inc_sw
        st_ref[0, gs] = s
        st_ref[1, gs] = sw
        for i, g in enumerate(groups):
            before = jnp.concatenate([cs[i] for cs in carried], axis=0).astype(BF16)
            o_ref[g] = rs[i][:, :SSM_FLAT] + jnp.dot(before, ms_ref[g], preferred_element_type=F32)

    for j in range(N_LANE_SLABS):
        rot = []
        for t in range(SSM_CHUNK):
            k = t % half
            x = step_rows(u, t, j)
            rot.append(x if k == 0 else pltpu.roll(x, SSM_GROUP * k, axis=1))
        for gg in range(GROUPS_PER_SLAB):
            for hf in range(2):
                acc = rot[half * hf]
                for k in range(1, half):
                    acc = jnp.where(in_slot[(gg + k) % half], rot[half * hf + k], acc)
                z_ref[j * GROUPS_PER_SLAB + gg, :, hf * LANES:(hf + 1) * LANES] = acc.astype(BF16)

    for j in range(N_LANE_SLABS):
        slab_outputs(j)

    for j in range(N_LANE_SLABS):
        for hf in range(2):
            ys = [o_ref[j * GROUPS_PER_SLAB + gg, :, hf * LANES:(hf + 1) * LANES]
                  for gg in range(GROUPS_PER_SLAB)]
            for k in range(half):
                acc = ys[0]
                for gg in range(1, GROUPS_PER_SLAB):
                    acc = jnp.where(in_slot[(gg + k) % half], ys[gg], acc)
                nat = acc if k == 0 else pltpu.roll(acc, LANES - SSM_GROUP * k, axis=1)
                t = half * hf + k
                for c in range(nc):
                    y_ref[c * chunk_rows + t * SUBLANES:c * chunk_rows + (t + 1) * SUBLANES,
                          j * LANES:(j + 1) * LANES] = nat[c * SUBLANES:(c + 1) * SUBLANES]

    y = _gelu_tanh(y_ref[...] + d_ref[...] * u)
    gate = jnp.dot(y.astype(BF16), wglu_ref[...], preferred_element_type=F32)
    ya_ref[...] = (y * _sigmoid(gate)).astype(BF16)

    @pl.when(ti == pl.num_programs(1) - 1)
    def _():
        sfin_ref[...] = st_ref[...]


def _s5(p, w1, ms, c1, c2, d_skip, w_glu, s0, tt):
    ng, rows, _ = p.shape
    m = tt // SSM_CHUNK * SUBLANES
    state_shape = (2, SSM_GROUPS, SUBLANES, SSM_STATE_LANES)
    return pl.pallas_call(
        functools.partial(_s5_kernel, tt=tt),
        grid=(ng, rows // (tt * SUBLANES)),
        in_specs=[
            _rows(tt, D_MODEL, 0),
            _resident(w1.shape),
            _resident(ms.shape),
            _resident(c1.shape),
            _resident(c2.shape),
            _resident((1, D_MODEL)),
            _resident((D_MODEL, D_MODEL)),
            pl.BlockSpec((None,) + state_shape, lambda g, t: (0, 0, 0, 0, 0)),
        ],
        out_specs=[
            _rows(tt, D_MODEL),
            pl.BlockSpec((None,) + state_shape, lambda g, t: (g, 0, 0, 0, 0)),
        ],
        out_shape=[
            jax.ShapeDtypeStruct((ng, rows, D_MODEL), BF16),
            jax.ShapeDtypeStruct((ng,) + state_shape, F32),
        ],
        scratch_shapes=[
            pltpu.VMEM(state_shape, F32),
            pltpu.VMEM((SSM_GROUPS, m, SSM_FLAT), BF16),
            pltpu.VMEM((SSM_GROUPS, m, SSM_FLAT), F32),
            pltpu.VMEM((tt * SUBLANES, D_MODEL), F32),
        ],
        compiler_params=_params(("parallel", "arbitrary")),
        name="s5",
    )(p, w1, ms, c1, c2, d_skip, w_glu, s0)


def _hgrn_kernel(q_ref, f_ref, i_ref, og_ref, lb_ref, ng_ref, s0_ref,
                 yb_ref, sfin_ref, st_ref, qin_ref, kout_ref, v_ref, oint_ref, *, tt, chunk, heads):
    ti = pl.program_id(2)

    @pl.when(ti == 0)
    def _():
        st_ref[...] = s0_ref[...]

    rows = chunk * SUBLANES
    r_idx = lax.broadcasted_iota(jnp.int32, (rows, rows), 0)
    c_idx = lax.broadcasted_iota(jnp.int32, (rows, rows), 1)
    allowed = (((r_idx - c_idx) & (SUBLANES - 1)) == 0) & (c_idx <= r_idx)
    mid = chunk // 2 - 1
    head = (chunk, SUBLANES, HGRN_HEAD_DIM)

    def decays(hh, ci):
        r = slice(ci * rows, (ci + 1) * rows)
        slot = ci % 2
        lanes = slice(hh * HGRN_HEAD_DIM, (hh + 1) * HGRN_HEAD_DIM)
        half_gap = 0.5 * (1.0 - lb_ref[:, lanes])
        swing = half_gap * jnp.tanh(0.5 * f_ref[r, lanes].astype(F32))
        f = (1.0 - half_gap) + swing
        k = (half_gap - swing).reshape(head)
        cum = jnp.log(f).reshape(head)
        shift = 1
        while shift < chunk:
            cum = cum + jnp.concatenate([jnp.zeros((shift,) + head[1:], F32), cum[:-shift]], axis=0)
            shift *= 2
        ref_row = cum[mid:mid + 1]
        last = cum[chunk - 1:chunk]
        q = q_ref[r, lanes].astype(F32).reshape(head)
        q_mid = q * jnp.exp(cum - ref_row)
        k_mid = k * jnp.exp(ref_row - cum)
        qin_ref[slot, hh] = (q_mid * jnp.exp(ref_row)).reshape(rows, HGRN_HEAD_DIM)
        kout_ref[slot, hh] = (k_mid * jnp.exp(last - ref_row)).reshape(rows, HGRN_HEAD_DIM)
        v = i_ref[r, lanes]
        v_ref[slot, hh] = v.astype(F32)
        return (q_mid.reshape(rows, HGRN_HEAD_DIM).astype(BF16),
                k_mid.reshape(rows, HGRN_HEAD_DIM).astype(BF16), v, jnp.exp(last[0]))

    pairs = [(hh, b) for hh in range(heads) for b in range(SUBLANES)]
    rb = [pl.ds(b, chunk, stride=SUBLANES) for b in range(SUBLANES)]

    def matmuls(ci, operands):
        r = slice(ci * rows, (ci + 1) * rows)
        slot = ci % 2
        q_mid, k_mid, v, decay = zip(*operands)
        scores = [lax.dot_general(q_mid[hh], k_mid[hh], (((1,), (1,)), ((), ())),
                                  preferred_element_type=F32) for hh in range(heads)]
        scores = [jnp.where(allowed, s, 0.0).astype(BF16) for s in scores]
        o = [jnp.dot(scores[hh], v[hh], preferred_element_type=F32) for hh in range(heads)]
        for hh, b in pairs:
            state_t = st_ref[hh, b]
            oint_ref[slot, hh, rb[b], :] = lax.dot_general(
                qin_ref[slot, hh, rb[b], :].astype(BF16), state_t.astype(BF16),
                (((1,), (1,)), ((), ())), preferred_element_type=F32)
            st_ref[hh, b] = state_t * decay[hh][b:b + 1, :] + lax.dot_general(
                v_ref[slot, hh, rb[b], :].astype(BF16), kout_ref[slot, hh, rb[b], :].astype(BF16),
                (((0,), (0,)), ((), ())), preferred_element_type=F32)
        for hh in range(heads):
            lanes = slice(hh * HGRN_HEAD_DIM, (hh + 1) * HGRN_HEAD_DIM)
            out = _rmsnorm(o[hh] + oint_ref[slot, hh], ng_ref[:, lanes])
            yb_ref[r, lanes] = (out * _silu(og_ref[r, lanes].astype(F32))).astype(BF16)

    n_chunks = tt // chunk
    operands = [decays(hh, 0) for hh in range(heads)]
    for ci in range(n_chunks):
        following = [decays(hh, ci + 1) for hh in range(heads)] if ci + 1 < n_chunks else None
        matmuls(ci, operands)
        operands = following

    @pl.when(ti == pl.num_programs(2) - 1)
    def _():
        sfin_ref[...] = st_ref[...]


def _hgrn(p, lb, ng_, s0, tt, chunk):
    ng, rows, _ = p.shape
    heads = HGRN_HEADS_PER_STEP
    width = heads * HGRN_HEAD_DIM
    steps = HGRN_HEADS // heads
    col = lambda k: pl.BlockSpec((None, tt * SUBLANES, width), lambda g, h, t, k=k: (g, t, k * steps + h))
    per_head = pl.BlockSpec((1, width), lambda g, h, t: (0, h))
    state_shape = (heads, SUBLANES, HGRN_HEAD_DIM, HGRN_HEAD_DIM)
    chunk_buf = pltpu.VMEM((2, heads, chunk * SUBLANES, HGRN_HEAD_DIM), F32)
    return pl.pallas_call(
        functools.partial(_hgrn_kernel, tt=tt, chunk=chunk, heads=heads),
        grid=(ng, steps, rows // (tt * SUBLANES)),
        in_specs=[col(1), col(2), col(3), col(4), per_head, per_head,
                  pl.BlockSpec((None,) + state_shape, lambda g, h, t: (0, h, 0, 0, 0))],
        out_specs=[
            pl.BlockSpec((None, tt * SUBLANES, width), lambda g, h, t: (g, t, h)),
            pl.BlockSpec((None,) + state_shape, lambda g, h, t: (g, h, 0, 0, 0)),
        ],
        out_shape=[
            jax.ShapeDtypeStruct((ng, rows, D_MODEL), BF16),
            jax.ShapeDtypeStruct((ng, HGRN_HEADS, SUBLANES, HGRN_HEAD_DIM, HGRN_HEAD_DIM), F32),
        ],
        scratch_shapes=[pltpu.VMEM(state_shape, F32), chunk_buf, chunk_buf, chunk_buf, chunk_buf],
        compiler_params=_params(("parallel", "parallel", "arbitrary")),
        name="hgrn2",
    )(p, p, p, p, lb, ng_, s0)


def _merge_kernel(h_ref, ga_ref, gb_ref, ya_ref, yb_ref, wa_ref, wb_ref, wo_ref, o_ref):
    a = jnp.dot(ya_ref[...], wa_ref[...], preferred_element_type=F32)
    b = jnp.dot(yb_ref[...], wb_ref[...], preferred_element_type=F32)
    merged = _sigmoid(ga_ref[...].astype(F32)) * a + _sigmoid(gb_ref[...].astype(F32)) * b
    o_ref[...] = h_ref[...] + jnp.dot(merged.astype(BF16), wo_ref[...], preferred_element_type=F32)


def _merge(h, p, ya, yb, wa, wb, wo, tt):
    ng, rows, _ = h.shape
    w = _resident((D_MODEL, D_MODEL))
    return pl.pallas_call(
        _merge_kernel,
        grid=(ng, rows // (tt * SUBLANES)),
        in_specs=[_rows(tt, D_MODEL), _rows(tt, D_MODEL, 5), _rows(tt, D_MODEL, 6),
                  _rows(tt, D_MODEL), _rows(tt, D_MODEL), w, w, w],
        out_specs=_rows(tt, D_MODEL),
        out_shape=jax.ShapeDtypeStruct((ng, rows, D_MODEL), F32),
        compiler_params=_params(("parallel", "parallel")),
        name="merge",
    )(h, p, p, ya, yb, wa, wb, wo)


def _ffn_kernel(h_ref, g_ref, wu_ref, cw_ref, cb_ref, wd_ref, gf_ref,
                c0_ref, o_ref, cfin_ref, carry_ref, act_ref, slab_ref, *, tt):
    ti = pl.program_id(1)

    @pl.when(ti == 0)
    def _():
        carry_ref[...] = c0_ref[...]

    rows = tt * SUBLANES
    tail = (CONV_WIDTH - 1) * SUBLANES
    h = h_ref[...]
    z = _rmsnorm(h, g_ref[...]).astype(BF16)

    def conv_up(n):
        cols = slice(n * FF_CHUNK, (n + 1) * FF_CHUNK)
        pre = jnp.dot(z, wu_ref[:, cols], preferred_element_type=F32)
        ext = jnp.concatenate([carry_ref[n], pre], axis=0)
        carry_ref[n] = ext[rows:rows + tail]
        out = cb_ref[:, cols]
        for j in range(CONV_WIDTH):
            out = out + ext[j * SUBLANES:j * SUBLANES + rows] * cw_ref[j:j + 1, cols]
        return out

    for n in range(N_FF_CHUNKS):
        gate = conv_up(n)
        val = conv_up(N_FF_CHUNKS + n)
        act_ref[:, n * FF_CHUNK:(n + 1) * FF_CHUNK] = (_silu(gate) * val).astype(BF16)

    down = jnp.dot(act_ref[...], wd_ref[...], preferred_element_type=F32)
    out = _rmsnorm(h + down, gf_ref[...])
    for j in range(N_LANE_SLABS):
        slab_ref[j] = out[:, j * LANES:(j + 1) * LANES]
    for b in range(SUBLANES):
        for j in range(N_LANE_SLABS):
            o_ref[b, :, j * LANES:(j + 1) * LANES] = slab_ref[j, pl.ds(b, tt, stride=SUBLANES), :]

    @pl.when(ti == pl.num_programs(1) - 1)
    def _():
        cfin_ref[...] = carry_ref[...]


def _ffn(h, g, wu, cw, cb, wd, gf, c0, tt):
    ng, rows, _ = h.shape
    l = rows // SUBLANES
    carry_shape = (2 * N_FF_CHUNKS, (CONV_WIDTH - 1) * SUBLANES, FF_CHUNK)
    return pl.pallas_call(
        functools.partial(_ffn_kernel, tt=tt),
        grid=(ng, l // tt),
        in_specs=[
            _rows(tt, D_MODEL),
            _resident((1, D_MODEL)),
            _resident(wu.shape), _resident(cw.shape), _resident(cb.shape), _resident(wd.shape),
            _resident((1, D_MODEL)),
            pl.BlockSpec((None,) + carry_shape, lambda g, t: (0, 0, 0, 0)),
        ],
        out_specs=[
            pl.BlockSpec((SUBLANES, tt, D_MODEL), lambda g, t: (g, t, 0)),
            pl.BlockSpec((None,) + carry_shape, lambda g, t: (g, 0, 0, 0)),
        ],
        out_shape=[
            jax.ShapeDtypeStruct((ng * SUBLANES, l, D_MODEL), F32),
            jax.ShapeDtypeStruct((ng,) + carry_shape, F32),
        ],
        scratch_shapes=[
            pltpu.VMEM(carry_shape, F32),
            pltpu.VMEM((tt * SUBLANES, D_FF), BF16),
            pltpu.VMEM((N_LANE_SLABS, tt * SUBLANES, LANES), F32),
        ],
        compiler_params=_params(("parallel", "arbitrary")),
        name="ffn",
    )(h, g, wu, cw, cb, wd, gf, c0)


def _slot_order(a, axis):
    half = SSM_CHUNK // 2
    shape = a.shape
    a = a.reshape((N_LANE_SLABS, GROUPS_PER_SLAB) + shape[1:axis] + (2, half) + shape[axis + 1:])
    parts = [jnp.roll(a[:, gg], gg, axis=axis + 1) for gg in range(GROUPS_PER_SLAB)]
    return jnp.stack(parts, axis=1).reshape(shape)


def _s5_params(lam_re, lam_im, log_dt, b_re, b_im, c_re, c_im):
    hp = lax.Precision.HIGHEST
    t_len = SSM_CHUNK
    dt = jnp.exp(log_dt)[:, None]
    mag = jnp.exp(lam_re * dt)
    ab_re = mag * jnp.cos(lam_im * dt)
    ab_im = mag * jnp.sin(lam_im * dt)
    den = lam_re * lam_re + lam_im * lam_im
    nr = ab_re - 1.0
    coef_re = (nr * lam_re + ab_im * lam_im) / den
    coef_im = (ab_im * lam_re - nr * lam_im) / den
    bb_re = coef_re[..., None] * b_re - coef_im[..., None] * b_im
    bb_im = coef_re[..., None] * b_im + coef_im[..., None] * b_re
    tau = jnp.arange(t_len + 1, dtype=F32)[None, :, None]
    pw_mag = jnp.exp((lam_re * dt)[:, None, :] * tau)
    pw_re = pw_mag * jnp.cos((lam_im * dt)[:, None, :] * tau)
    pw_im = pw_mag * jnp.sin((lam_im * dt)[:, None, :] * tau)
    ab_b_re = pw_re[..., None] * bb_re[:, None] - pw_im[..., None] * bb_im[:, None]
    ab_b_im = pw_re[..., None] * bb_im[:, None] + pw_im[..., None] * bb_re[:, None]
    kern = (jnp.einsum("gop,gtpi->gtoi", c_re, ab_b_re[:, :t_len], precision=hp)
            - jnp.einsum("gop,gtpi->gtoi", c_im, ab_b_im[:, :t_len], precision=hp))
    kern = kern.astype(BF16)
    periodic = jnp.concatenate([kern, jnp.zeros_like(kern)], axis=1)
    toep = jnp.tile(periodic, (1, t_len, 1, 1))[:, :t_len * (2 * t_len - 1)]
    toep = toep.reshape(SSM_GROUPS, t_len, 2 * t_len - 1, SSM_GROUP, SSM_GROUP)[:, :, :t_len]
    toep = toep.transpose(0, 1, 4, 2, 3)
    inc_re = ab_b_re[:, t_len - 1::-1].transpose(0, 1, 3, 2).astype(BF16)
    inc_im = ab_b_im[:, t_len - 1::-1].transpose(0, 1, 3, 2).astype(BF16)
    inc = jnp.concatenate([inc_re, inc_im, inc_im, inc_re], axis=-1)
    out_re = (c_re[:, None] * pw_re[:, 1:, None, :] - c_im[:, None] * pw_im[:, 1:, None, :])
    out_im = -(c_re[:, None] * pw_im[:, 1:, None, :] + c_im[:, None] * pw_re[:, 1:, None, :])
    ms = jnp.concatenate([out_re, out_im], axis=-1).astype(BF16).transpose(0, 3, 1, 2)
    toep = _slot_order(_slot_order(toep, 1), 3).reshape(SSM_GROUPS, SSM_FLAT, SSM_FLAT)
    inc = _slot_order(inc, 1).reshape(SSM_GROUPS, SSM_FLAT, 2 * SSM_STATE_LANES)
    ms = _slot_order(ms, 2).reshape(SSM_GROUPS, SSM_STATE_LANES, SSM_FLAT)
    w1 = jnp.concatenate([toep, inc], axis=-1)
    a16_re, a16_im = pw_re[:, t_len], pw_im[:, t_len]
    c1 = jnp.concatenate([a16_re, a16_re], axis=-1)
    c2 = jnp.concatenate([-a16_im, a16_im], axis=-1)
    bcast = lambda a: jnp.broadcast_to(a[:, None, :], (SSM_GROUPS, SUBLANES, SSM_STATE_LANES))
    return w1, ms, bcast(c1), bcast(c2)


def _block(x, states, w):
    l = x.shape[1]
    s5_state, hgrn_state, conv_carry = states
    t_in, t_s5, t_hgrn, t_merge, t_ffn = (min(t, l) for t in TILES)
    p, h = _inproj(x, w["mix_g"], w["w_in"], t_in)
    ya, s5_state = _s5(p, w["s5_w1"], w["s5_ms"], w["s5_c1"], w["s5_c2"], w["d_skip"], w["w_glu"],
                       s5_state, t_s5)
    yb, hgrn_state = _hgrn(p, w["lb"], w["hgrn_g"], hgrn_state, t_hgrn, min(HGRN_CHUNK, l))
    h1 = _merge(h, p, ya, yb, w["w_a"], w["w_b"], w["w_out"], t_merge)
    out, conv_carry = _ffn(h1, w["ffn_g"], w["w_up"], w["conv_w"], w["conv_b"], w["w_down"],
                           w["final_g"], conv_carry, t_ffn)
    return out, (s5_state, hgrn_state, conv_carry)


def kernel(x, meta_tokens, mix_norm_g, w_in, ssm_lambda_re, ssm_lambda_im, ssm_log_dt, ssm_b_re, ssm_b_im, ssm_c_re, ssm_c_im, ssm_d, ssm_w_glu, w_ssm_proj, hgrn_lb_logits, hgrn_norm_g, w_hgrn_proj, w_out, ffn_norm_g, w_up, conv_w, conv_b, w_down, final_norm_g):
    assert x.shape[0] % SUBLANES == 0 and x.shape[1] % SSM_CHUNK == 0 and x.shape[2] == D_MODEL
    s5_w1, s5_ms, s5_c1, s5_c2 = _s5_params(ssm_lambda_re[0], ssm_lambda_im[0], ssm_log_dt[0],
                                            ssm_b_re[0], ssm_b_im[0], ssm_c_re[0], ssm_c_im[0])
    lower_bounds = jnp.cumsum(jax.nn.softmax(hgrn_lb_logits.astype(F32), axis=0), axis=0)
    w = {
        "mix_g": mix_norm_g[0].reshape(1, D_MODEL),
        "w_in": w_in[0].astype(BF16),
        "s5_w1": s5_w1, "s5_ms": s5_ms, "s5_c1": s5_c1, "s5_c2": s5_c2,
        "d_skip": ssm_d[0].reshape(1, D_MODEL),
        "w_glu": ssm_w_glu[0].astype(BF16),
        "lb": lower_bounds[0].reshape(1, D_MODEL),
        "hgrn_g": jnp.tile(hgrn_norm_g[0], HGRN_HEADS).reshape(1, D_MODEL),
        "w_a": w_ssm_proj[0].astype(BF16),
        "w_b": w_hgrn_proj[0].astype(BF16),
        "w_out": w_out[0].astype(BF16),
        "ffn_g": ffn_norm_g[0].reshape(1, D_MODEL),
        "w_up": w_up[0].astype(BF16),
        "conv_w": conv_w[0],
        "conv_b": conv_b[0].reshape(1, 2 * D_FF),
        "w_down": w_down[0].astype(BF16),
        "final_g": final_norm_g.reshape(1, D_MODEL),
    }
    zero_states = (
        jnp.zeros((1, 2, SSM_GROUPS, SUBLANES, SSM_STATE_LANES), F32),
        jnp.zeros((1, HGRN_HEADS, SUBLANES, HGRN_HEAD_DIM, HGRN_HEAD_DIM), F32),
        jnp.zeros((1, 2 * N_FF_CHUNKS, (CONV_WIDTH - 1) * SUBLANES, FF_CHUNK), F32),
    )
    meta = jnp.broadcast_to(meta_tokens.astype(x.dtype)[None], (SUBLANES, N_META, D_MODEL))
    _, states = _block(meta, zero_states, w)
    out, _ = _block(x, states, w)
    return out
```

```python
import functools
import math

import numpy as np
import jax
import jax.numpy as jnp
from jax import lax
from jax.experimental import pallas as pl
from jax.experimental.pallas import tpu as pltpu

F32 = jnp.float32
BF16 = jnp.bfloat16

D_MODEL = 1024
N_META = 16
SSM_GROUPS = 64
SSM_GROUP = 16
SSM_STATE = 64
SSM_CHUNK = 16
SSM_FLAT = SSM_CHUNK * SSM_GROUP
SSM_STATE_LANES = 2 * SSM_STATE
HGRN_HEADS = 8
HGRN_HEAD_DIM = 128
HGRN_CHUNK = 32
HGRN_HEADS_PER_STEP = 2
D_FF = 2816
FF_CHUNK = 256
N_FF_CHUNKS = D_FF // FF_CHUNK
CONV_WIDTH = 3
IN_BLOCKS = 7
EPS = 1e-6
GELU_C0 = math.sqrt(2.0 / math.pi)
GELU_C1 = 0.044715 * GELU_C0
LANES = 128
SUBLANES = 8
N_LANE_SLABS = D_MODEL // LANES
GROUPS_PER_SLAB = LANES // SSM_GROUP
VMEM_LIMIT_BYTES = 56 * 1024 * 1024
TILES = (32, 128, 256, 64, 64)


def _rmsnorm(x, g):
    return x * lax.rsqrt(jnp.mean(x * x, axis=-1, keepdims=True) + EPS) * g


def _sigmoid(x):
    return 0.5 * jnp.tanh(0.5 * x) + 0.5


def _silu(x):
    h = 0.5 * x
    return h + h * jnp.tanh(h)


def _gelu_tanh(x):
    h = 0.5 * x
    return h + h * jnp.tanh(x * (GELU_C0 + GELU_C1 * (x * x)))


def _resident(shape):
    n = len(shape)
    return pl.BlockSpec(shape, lambda *_: (0,) * n, pipeline_mode=pl.Buffered(1))


def _params(semantics):
    return pltpu.CompilerParams(dimension_semantics=semantics, vmem_limit_bytes=VMEM_LIMIT_BYTES)


def _rows(tt, cols, k=0):
    return pl.BlockSpec((None, tt * SUBLANES, cols), lambda g, t, k=k: (g, t, k))


def _inproj_kernel(x_ref, g_ref, w_ref, p_ref, h_ref, slab_ref, *, tt):
    for b in range(SUBLANES):
        for j in range(N_LANE_SLABS):
            slab_ref[j, pl.ds(b, tt, stride=SUBLANES), :] = x_ref[b, :, j * LANES:(j + 1) * LANES]
    h = jnp.concatenate([slab_ref[j] for j in range(N_LANE_SLABS)], axis=1)
    h_ref[...] = h
    z = _rmsnorm(h, g_ref[...]).astype(BF16)
    for k in range(IN_BLOCKS):
        cols = slice(k * D_MODEL, (k + 1) * D_MODEL)
        p_ref[:, cols] = jnp.dot(z, w_ref[:, cols], preferred_element_type=F32).astype(BF16)


def _inproj(x, g, w, tt):
    b, l, _ = x.shape
    ng = b // SUBLANES
    return pl.pallas_call(
        functools.partial(_inproj_kernel, tt=tt),
        grid=(ng, l // tt),
        in_specs=[
            pl.BlockSpec((SUBLANES, tt, D_MODEL), lambda i, t: (i, t, 0)),
            _resident((1, D_MODEL)),
            _resident((D_MODEL, IN_BLOCKS * D_MODEL)),
        ],
        out_specs=[_rows(tt, IN_BLOCKS * D_MODEL), _rows(tt, D_MODEL)],
        out_shape=[
            jax.ShapeDtypeStruct((ng, l * SUBLANES, IN_BLOCKS * D_MODEL), BF16),
            jax.ShapeDtypeStruct((ng, l * SUBLANES, D_MODEL), F32),
        ],
        scratch_shapes=[pltpu.VMEM((N_LANE_SLABS, tt * SUBLANES, LANES), F32)],
        compiler_params=_params(("parallel", "parallel")),
        name="inproj",
    )(x, g, w)


def _s5_kernel(u_ref, w1_ref, ms_ref, c1_ref, c2_ref, d_ref, wglu_ref, s0_ref,
               ya_ref, sfin_ref, st_ref, z_ref, o_ref, y_ref, *, tt):
    ti = pl.program_id(1)

    @pl.when(ti == 0)
    def _():
        st_ref[...] = s0_ref[...]

    nc = tt // SSM_CHUNK
    m = nc * SUBLANES
    chunk_rows = SSM_CHUNK * SUBLANES
    half = SSM_CHUNK // 2
    lane_slot = lax.broadcasted_iota(jnp.int32, (m, LANES), 1) // SSM_GROUP
    in_slot = [lane_slot == s for s in range(GROUPS_PER_SLAB)]

    u = u_ref[...].astype(F32)

    def step_rows(x, t, j):
        return jnp.concatenate(
            [x[c * chunk_rows + t * SUBLANES:c * chunk_rows + (t + 1) * SUBLANES, j * LANES:(j + 1) * LANES]
             for c in range(nc)], axis=0)

    def slab_outputs(j):
        groups = range(j * GROUPS_PER_SLAB, (j + 1) * GROUPS_PER_SLAB)
        gs = slice(j * GROUPS_PER_SLAB, (j + 1) * GROUPS_PER_SLAB)
        rs = [jnp.dot(z_ref[g], w1_ref[g], preferred_element_type=F32) for g in groups]
        s, sw = st_ref[0, gs], st_ref[1, gs]
        c1, c2 = c1_ref[gs], c2_ref[gs]
        carried = []
        for c in range(nc):
            carried.append(s)
            rows_c = slice(c * SUBLANES, (c + 1) * SUBLANES)
            inc = jnp.stack([r[rows_c, SSM_FLAT:SSM_FLAT + SSM_STATE_LANES] for r in rs])
            inc_sw = jnp.stack([r[rows_c, SSM_FLAT + SSM_STATE_LANES:] for r in rs])
            s, sw = c1 * s + c2 * sw + inc, c1 * sw - c2 * s + inc_sw
        st_ref[0, gs] = s
        st_ref[1, gs] = sw
        for i, g in enumerate(groups):
            before = jnp.concatenate([cs[i] for cs in carried], axis=0).astype(BF16)
            o_ref[g] = rs[i][:, :SSM_FLAT] + jnp.dot(before, ms_ref[g], preferred_element_type=F32)

    for j in range(N_LANE_SLABS):
        rot = []
        for t in range(SSM_CHUNK):
            k = t % half
            x = step_rows(u, t, j)
            rot.append(x if k == 0 else pltpu.roll(x, SSM_GROUP * k, axis=1))
        for gg in range(GROUPS_PER_SLAB):
            for hf in range(2):
                acc = rot[half * hf]
                for k in range(1, half):
                    acc = jnp.where(in_slot[(gg + k) % half], rot[half * hf + k], acc)
                z_ref[j * GROUPS_PER_SLAB + gg, :, hf * LANES:(hf + 1) * LANES] = acc.astype(BF16)

    for j in range(N_LANE_SLABS):
        slab_outputs(j)

    for j in range(N_LANE_SLABS):
        for hf in range(2):
            ys = [o_ref[j * GROUPS_PER_SLAB + gg, :, hf * LANES:(hf + 1) * LANES]
                  for gg in range(GROUPS_PER_SLAB)]
            for k in range(half):
                acc = ys[0]
                for gg in range(1, GROUPS_PER_SLAB):
                    acc = jnp.where(in_slot[(gg + k) % half], ys[gg], acc)
                nat = acc if k == 0 else pltpu.roll(acc, LANES - SSM_GROUP * k, axis=1)
                t = half * hf + k
                for c in range(nc):
                    y_ref[c * chunk_rows + t * SUBLANES:c * chunk_rows + (t + 1) * SUBLANES,
                          j * LANES:(j + 1) * LANES] = nat[c * SUBLANES:(c + 1) * SUBLANES]

    y = _gelu_tanh(y_ref[...] + d_ref[...] * u)
    gate = jnp.dot(y.astype(BF16), wglu_ref[...], preferred_element_type=F32)
    ya_ref[...] = (y * _sigmoid(gate)).astype(BF16)

    @pl.when(ti == pl.num_programs(1) - 1)
    def _():
        sfin_ref[...] = st_ref[...]


def _s5(p, w1, ms, c1, c2, d_skip, w_glu, s0, tt):
    ng, rows, _ = p.shape
    m = tt // SSM_CHUNK * SUBLANES
    state_shape = (2, SSM_GROUPS, SUBLANES, SSM_STATE_LANES)
    return pl.pallas_call(
        functools.partial(_s5_kernel, tt=tt),
        grid=(ng, rows // (tt * SUBLANES)),
        in_specs=[
            _rows(tt, D_MODEL, 0),
            _resident(w1.shape),
            _resident(ms.shape),
            _resident(c1.shape),
            _resident(c2.shape),
            _resident((1, D_MODEL)),
            _resident((D_MODEL, D_MODEL)),
            pl.BlockSpec((None,) + state_shape, lambda g, t: (0, 0, 0, 0, 0)),
        ],
        out_specs=[
            _rows(tt, D_MODEL),
            pl.BlockSpec((None,) + state_shape, lambda g, t: (g, 0, 0, 0, 0)),
        ],
        out_shape=[
            jax.ShapeDtypeStruct((ng, rows, D_MODEL), BF16),
            jax.ShapeDtypeStruct((ng,) + state_shape, F32),
        ],
        scratch_shapes=[
            pltpu.VMEM(state_shape, F32),
            pltpu.VMEM((SSM_GROUPS, m, SSM_FLAT), BF16),
            pltpu.VMEM((SSM_GROUPS, m, SSM_FLAT), F32),
            pltpu.VMEM((tt * SUBLANES, D_MODEL), F32),
        ],
        compiler_params=_params(("parallel", "arbitrary")),
        name="s5",
    )(p, w1, ms, c1, c2, d_skip, w_glu, s0)


def _hgrn_kernel(q_ref, f_ref, i_ref, og_ref, lb_ref, ng_ref, s0_ref,
                 yb_ref, sfin_ref, st_ref, qin_ref, kout_ref, v_ref, oint_ref, *, tt, chunk, heads):
    ti = pl.program_id(2)

    @pl.when(ti == 0)
    def _():
        st_ref[...] = s0_ref[...]

    rows = chunk * SUBLANES
    r_idx = lax.broadcasted_iota(jnp.int32, (rows, rows), 0)
    c_idx = lax.broadcasted_iota(jnp.int32, (rows, rows), 1)
    allowed = (((r_idx - c_idx) & (SUBLANES - 1)) == 0) & (c_idx <= r_idx)
    mid = chunk // 2 - 1
    head = (chunk, SUBLANES, HGRN_HEAD_DIM)

    def decays(hh, ci):
        r = slice(ci * rows, (ci + 1) * rows)
        slot = ci % 2
        lanes = slice(hh * HGRN_HEAD_DIM, (hh + 1) * HGRN_HEAD_DIM)
        half_gap = 0.5 * (1.0 - lb_ref[:, lanes])
        swing = half_gap * jnp.tanh(0.5 * f_ref[r, lanes].astype(F32))
        f = (1.0 - half_gap) + swing
        k = (half_gap - swing).reshape(head)
        cum = jnp.log(f).reshape(head)
        shift = 1
        while shift < chunk:
            cum = cum + jnp.concatenate([jnp.zeros((shift,) + head[1:], F32), cum[:-shift]], axis=0)
            shift *= 2
        ref_row = cum[mid:mid + 1]
        last = cum[chunk - 1:chunk]
        q = q_ref[r, lanes].astype(F32).reshape(head)
        q_mid = q * jnp.exp(cum - ref_row)
        k_mid = k * jnp.exp(ref_row - cum)
        qin_ref[slot, hh] = (q_mid * jnp.exp(ref_row)).reshape(rows, HGRN_HEAD_DIM)
        kout_ref[slot, hh] = (k_mid * jnp.exp(last - ref_row)).reshape(rows, HGRN_HEAD_DIM)
        v = i_ref[r, lanes]
        v_ref[slot, hh] = v.astype(F32)
        return (q_mid.reshape(rows, HGRN_HEAD_DIM).astype(BF16),
                k_mid.reshape(rows, HGRN_HEAD_DIM).astype(BF16), v, jnp.exp(last[0]))

    pairs = [(hh, b) for hh in range(heads) for b in range(SUBLANES)]
    rb = [pl.ds(b, chunk, stride=SUBLANES) for b in range(SUBLANES)]

    def matmuls(ci, operands):
        r = slice(ci * rows, (ci + 1) * rows)
        slot = ci % 2
        q_mid, k_mid, v, decay = zip(*operands)
        scores = [lax.dot_general(q_mid[hh], k_mid[hh], (((1,), (1,)), ((), ())),
                                  preferred_element_type=F32) for hh in range(heads)]
        scores = [jnp.where(allowed, s, 0.0).astype(BF16) for s in scores]
        o = [jnp.dot(scores[hh], v[hh], preferred_element_type=F32) for hh in range(heads)]
        for hh, b in pairs:
            state_t = st_ref[hh, b]
            oint_ref[slot, hh, rb[b], :] = lax.dot_general(
                qin_ref[slot, hh, rb[b], :].astype(BF16), state_t.astype(BF16),
                (((1,), (1,)), ((), ())), preferred_element_type=F32)
            st_ref[hh, b] = state_t * decay[hh][b:b + 1, :] + lax.dot_general(
                v_ref[slot, hh, rb[b], :].astype(BF16), kout_ref[slot, hh, rb[b], :].astype(BF16),
                (((0,), (0,)), ((), ())), preferred_element_type=F32)
        for hh in range(heads):
            lanes = slice(hh * HGRN_HEAD_DIM, (hh + 1) * HGRN_HEAD_DIM)
            out = _rmsnorm(o[hh] + oint_ref[slot, hh], ng_ref[:, lanes])
            yb_ref[r, lanes] = (out * _silu(og_ref[r, lanes].astype(F32))).astype(BF16)

    n_chunks = tt // chunk
    operands = [decays(hh, 0) for hh in range(heads)]
    for ci in range(n_chunks):
        following = [decays(hh, ci + 1) for hh in range(heads)] if ci + 1 < n_chunks else None
        matmuls(ci, operands)
        operands = following

    @pl.when(ti == pl.num_programs(2) - 1)
    def _():
        sfin_ref[...] = st_ref[...]


def _hgrn(p, lb, ng_, s0, tt, chunk):
    ng, rows, _ = p.shape
    heads = HGRN_HEADS_PER_STEP
    width = heads * HGRN_HEAD_DIM
    steps = HGRN_HEADS // heads
    col = lambda k: pl.BlockSpec((None, tt * SUBLANES, width), lambda g, h, t, k=k: (g, t, k * steps + h))
    per_head = pl.BlockSpec((1, width), lambda g, h, t: (0, h))
    state_shape = (heads, SUBLANES, HGRN_HEAD_DIM, HGRN_HEAD_DIM)
    chunk_buf = pltpu.VMEM((2, heads, chunk * SUBLANES, HGRN_HEAD_DIM), F32)
    return pl.pallas_call(
        functools.partial(_hgrn_kernel, tt=tt, chunk=chunk, heads=heads),
        grid=(ng, steps, rows // (tt * SUBLANES)),
        in_specs=[col(1), col(2), col(3), col(4), per_head, per_head,
                  pl.BlockSpec((None,) + state_shape, lambda g, h, t: (0, h, 0, 0, 0))],
        out_specs=[
            pl.BlockSpec((None, tt * SUBLANES, width), lambda g, h, t: (g, t, h)),
            pl.BlockSpec((None,) + state_shape, lambda g, h, t: (g, h, 0, 0, 0)),
        ],
        out_shape=[
            jax.ShapeDtypeStruct((ng, rows, D_MODEL), BF16),
            jax.ShapeDtypeStruct((ng, HGRN_HEADS, SUBLANES, HGRN_HEAD_DIM, HGRN_HEAD_DIM), F32),
        ],
        scratch_shapes=[pltpu.VMEM(state_shape, F32), chunk_buf, chunk_buf, chunk_buf, chunk_buf],
        compiler_params=_params(("parallel", "parallel", "arbitrary")),
        name="hgrn2",
    )(p, p, p, p, lb, ng_, s0)


def _merge_kernel(h_ref, ga_ref, gb_ref, ya_ref, yb_ref, wa_ref, wb_ref, wo_ref, o_ref):
    a = jnp.dot(ya_ref[...], wa_ref[...], preferred_element_type=F32)
    b = jnp.dot(yb_ref[...], wb_ref[...], preferred_element_type=F32)
    merged = _sigmoid(ga_ref[...].astype(F32)) * a + _sigmoid(gb_ref[...].astype(F32)) * b
    o_ref[...] = h_ref[...] + jnp.dot(merged.astype(BF16), wo_ref[...], preferred_element_type=F32)


def _merge(h, p, ya, yb, wa, wb, wo, tt):
    ng, rows, _ = h.shape
    w = _resident((D_MODEL, D_MODEL))
    return pl.pallas_call(
        _merge_kernel,
        grid=(ng, rows // (tt * SUBLANES)),
        in_specs=[_rows(tt, D_MODEL), _rows(tt, D_MODEL, 5), _rows(tt, D_MODEL, 6),
                  _rows(tt, D_MODEL), _rows(tt, D_MODEL), w, w, w],
        out_specs=_rows(tt, D_MODEL),
        out_shape=jax.ShapeDtypeStruct((ng, rows, D_MODEL), F32),
        compiler_params=_params(("parallel", "parallel")),
        name="merge",
    )(h, p, p, ya, yb, wa, wb, wo)


def _ffn_kernel(h_ref, g_ref, wu_ref, cw_ref, cb_ref, wd_ref, gf_ref,
                c0_ref, o_ref, cfin_ref, carry_ref, act_ref, slab_ref, *, tt):
    ti = pl.program_id(1)

    @pl.when(ti == 0)
    def _():
        carry_ref[...] = c0_ref[...]

    rows = tt * SUBLANES
    tail = (CONV_WIDTH - 1) * SUBLANES
    h = h_ref[...]
    z = _rmsnorm(h, g_ref[...]).astype(BF16)

    def conv_up(n):
        cols = slice(n * FF_CHUNK, (n + 1) * FF_CHUNK)
        pre = jnp.dot(z, wu_ref[:, cols], preferred_element_type=F32)
        ext = jnp.concatenate([carry_ref[n], pre], axis=0)
        carry_ref[n] = ext[rows:rows + tail]
        out = cb_ref[:, cols]
        for j in range(CONV_WIDTH):
            out = out + ext[j * SUBLANES:j * SUBLANES + rows] * cw_ref[j:j + 1, cols]
        return out

    for n in range(N_FF_CHUNKS):
        gate = conv_up(n)
        val = conv_up(N_FF_CHUNKS + n)
        act_ref[:, n * FF_CHUNK:(n + 1) * FF_CHUNK] = (_silu(gate) * val).astype(BF16)

    down = jnp.dot(act_ref[...], wd_ref[...], preferred_element_type=F32)
    out = _rmsnorm(h + down, gf_ref[...])
    for j in range(N_LANE_SLABS):
        slab_ref[j] = out[:, j * LANES:(j + 1) * LANES]
    for b in range(SUBLANES):
        for j in range(N_LANE_SLABS):
            o_ref[b, :, j * LANES:(j + 1) * LANES] = slab_ref[j, pl.ds(b, tt, stride=SUBLANES), :]

    @pl.when(ti == pl.num_programs(1) - 1)
    def _():
        cfin_ref[...] = carry_ref[...]


def _ffn(h, g, wu, cw, cb, wd, gf, c0, tt):
    ng, rows, _ = h.shape
    l = rows // SUBLANES
    carry_shape = (2 * N_FF_CHUNKS, (CONV_WIDTH - 1) * SUBLANES, FF_CHUNK)
    return pl.pallas_call(
        functools.partial(_ffn_kernel, tt=tt),
        grid=(ng, l // tt),
        in_specs=[
            _rows(tt, D_MODEL),
            _resident((1, D_MODEL)),
            _resident(wu.shape), _resident(cw.shape), _resident(cb.shape), _resident(wd.shape),
            _resident((1, D_MODEL)),
            pl.BlockSpec((None,) + carry_shape, lambda g, t: (0, 0, 0, 0)),
        ],
        out_specs=[
            pl.BlockSpec((SUBLANES, tt, D_MODEL), lambda g, t: (g, t, 0)),
            pl.BlockSpec((None,) + carry_shape, lambda g, t: (g, 0, 0, 0)),
        ],
        out_shape=[
            jax.ShapeDtypeStruct((ng * SUBLANES, l, D_MODEL), F32),
            jax.ShapeDtypeStruct((ng,) + carry_shape, F32),
        ],
        scratch_shapes=[
            pltpu.VMEM(carry_shape, F32),
            pltpu.VMEM((tt * SUBLANES, D_FF), BF16),
            pltpu.VMEM((N_LANE_SLABS, tt * SUBLANES, LANES), F32),
        ],
        compiler_params=_params(("parallel", "arbitrary")),
        name="ffn",
    )(h, g, wu, cw, cb, wd, gf, c0)


def _s5_params(lam_re, lam_im, log_dt, b_re, b_im, c_re, c_im):
    hp = lax.Precision.HIGHEST
    t_len = SSM_CHUNK
    dt = jnp.exp(log_dt)[:, None]
    mag = jnp.exp(lam_re * dt)
    ab_re = mag * jnp.cos(lam_im * dt)
    ab_im = mag * jnp.sin(lam_im * dt)
    den = lam_re * lam_re + lam_im * lam_im
    nr = ab_re - 1.0
    coef_re = (nr * lam_re + ab_im * lam_im) / den
    coef_im = (ab_im * lam_re - nr * lam_im) / den
    bb_re = coef_re[..., None] * b_re - coef_im[..., None] * b_im
    bb_im = coef_re[..., None] * b_im + coef_im[..., None] * b_re
    tau = jnp.arange(t_len + 1, dtype=F32)[None, :, None]
    pw_mag = jnp.exp((lam_re * dt)[:, None, :] * tau)
    pw_re = pw_mag * jnp.cos((lam_im * dt)[:, None, :] * tau)
    pw_im = pw_mag * jnp.sin((lam_im * dt)[:, None, :] * tau)
    ab_b_re = pw_re[..., None] * bb_re[:, None] - pw_im[..., None] * bb_im[:, None]
    ab_b_im = pw_re[..., None] * bb_im[:, None] + pw_im[..., None] * bb_re[:, None]
    kern = (jnp.einsum("gop,gtpi->gtoi", c_re, ab_b_re[:, :t_len], precision=hp)
            - jnp.einsum("gop,gtpi->gtoi", c_im, ab_b_im[:, :t_len], precision=hp))
    out_re = c_re[:, None] * pw_re[:, :, None, :] - c_im[:, None] * pw_im[:, :, None, :]
    out_im = -(c_re[:, None] * pw_im[:, :, None, :] + c_im[:, None] * pw_re[:, :, None, :])

    half = t_len // 2
    slots = np.arange(t_len)
    step_of = np.stack([half * (slots // half) + (slots % half - gg) % half
                        for gg in range(GROUPS_PER_SLAB)])
    lags = np.arange(t_len + 1)
    lag_sel = (step_of[:, None, None, :] - step_of[:, None, :, None]) == lags[None, :t_len, None, None]
    inc_sel = (t_len - 1 - step_of[:, :, None]) == lags[None, None, :t_len]
    out_sel = (step_of[:, :, None] + 1) == lags[None, None, :]
    by_slab = lambda a: a.astype(BF16).reshape((N_LANE_SLABS, GROUPS_PER_SLAB) + a.shape[1:])
    select = lambda spec, sel, a: jnp.einsum(spec, jnp.asarray(sel, BF16), by_slab(a),
                                             preferred_element_type=F32).astype(BF16)
    toep = select("xurc,jxuoi->jxrico", lag_sel, kern).reshape(SSM_GROUPS, SSM_FLAT, SSM_FLAT)
    inc_re = select("xsu,jxuph->jxshp", inc_sel, ab_b_re[:, :t_len])
    inc_im = select("xsu,jxuph->jxshp", inc_sel, ab_b_im[:, :t_len])
    inc = jnp.concatenate([inc_re, inc_im, inc_im, inc_re], axis=-1)
    w1 = jnp.concatenate([toep, inc.reshape(SSM_GROUPS, SSM_FLAT, 2 * SSM_STATE_LANES)], axis=-1)
    ms = jnp.concatenate([select("xsu,jxuop->jxpso", out_sel, out_re),
                          select("xsu,jxuop->jxpso", out_sel, out_im)], axis=2)
    ms = ms.reshape(SSM_GROUPS, SSM_STATE_LANES, SSM_FLAT)
    a16_re, a16_im = pw_re[:, t_len], pw_im[:, t_len]
    c1 = jnp.concatenate([a16_re, a16_re], axis=-1)
    c2 = jnp.concatenate([-a16_im, a16_im], axis=-1)
    bcast = lambda a: jnp.broadcast_to(a[:, None, :], (SSM_GROUPS, SUBLANES, SSM_STATE_LANES))
    return w1, ms, bcast(c1), bcast(c2)


def _block(x, states, w):
    l = x.shape[1]
    s5_state, hgrn_state, conv_carry = states
    t_in, t_s5, t_hgrn, t_merge, t_ffn = (min(t, l) for t in TILES)
    p, h = _inproj(x, w["mix_g"], w["w_in"], t_in)
    ya, s5_state = _s5(p, w["s5_w1"], w["s5_ms"], w["s5_c1"], w["s5_c2"], w["d_skip"], w["w_glu"],
                       s5_state, t_s5)
    yb, hgrn_state = _hgrn(p, w["lb"], w["hgrn_g"], hgrn_state, t_hgrn, min(HGRN_CHUNK, l))
    h1 = _merge(h, p, ya, yb, w["w_a"], w["w_b"], w["w_out"], t_merge)
    out, conv_carry = _ffn(h1, w["ffn_g"], w["w_up"], w["conv_w"], w["conv_b"], w["w_down"],
                           w["final_g"], conv_carry, t_ffn)
    return out, (s5_state, hgrn_state, conv_carry)


def kernel(x, meta_tokens, mix_norm_g, w_in, ssm_lambda_re, ssm_lambda_im, ssm_log_dt, ssm_b_re, ssm_b_im, ssm_c_re, ssm_c_im, ssm_d, ssm_w_glu, w_ssm_proj, hgrn_lb_logits, hgrn_norm_g, w_hgrn_proj, w_out, ffn_norm_g, w_up, conv_w, conv_b, w_down, final_norm_g):
    assert x.shape[0] % SUBLANES == 0 and x.shape[1] % SSM_CHUNK == 0 and x.shape[2] == D_MODEL
    s5_w1, s5_ms, s5_c1, s5_c2 = _s5_params(ssm_lambda_re[0], ssm_lambda_im[0], ssm_log_dt[0],
                                            ssm_b_re[0], ssm_b_im[0], ssm_c_re[0], ssm_c_im[0])
    lower_bounds = jnp.cumsum(jax.nn.softmax(hgrn_lb_logits.astype(F32), axis=0), axis=0)
    w = {
        "mix_g": mix_norm_g[0].reshape(1, D_MODEL),
        "w_in": w_in[0].astype(BF16),
        "s5_w1": s5_w1, "s5_ms": s5_ms, "s5_c1": s5_c1, "s5_c2": s5_c2,
        "d_skip": ssm_d[0].reshape(1, D_MODEL),
        "w_glu": ssm_w_glu[0].astype(BF16),
        "lb": lower_bounds[0].reshape(1, D_MODEL),
        "hgrn_g": jnp.tile(hgrn_norm_g[0], HGRN_HEADS).reshape(1, D_MODEL),
        "w_a": w_ssm_proj[0].astype(BF16),
        "w_b": w_hgrn_proj[0].astype(BF16),
        "w_out": w_out[0].astype(BF16),
        "ffn_g": ffn_norm_g[0].reshape(1, D_MODEL),
        "w_up": w_up[0].astype(BF16),
        "conv_w": conv_w[0],
        "conv_b": conv_b[0].reshape(1, 2 * D_FF),
        "w_down": w_down[0].astype(BF16),
        "final_g": final_norm_g.reshape(1, D_MODEL),
    }
    zero_states = (
        jnp.zeros((1, 2, SSM_GROUPS, SUBLANES, SSM_STATE_LANES), F32),
        jnp.zeros((1, HGRN_HEADS, SUBLANES, HGRN_HEAD_DIM, HGRN_HEAD_DIM), F32),
        jnp.zeros((1, 2 * N_FF_CHUNKS, (CONV_WIDTH - 1) * SUBLANES, FF_CHUNK), F32),
    )
    meta = jnp.broadcast_to(meta_tokens.astype(x.dtype)[None], (SUBLANES, N_META, D_MODEL))
    _, states = _block(meta, zero_states, w)
    out, _ = _block(x, states, w)
    return out
```

```python
import functools
import math

import numpy as np
import jax
import jax.numpy as jnp
from jax import lax
from jax.experimental import pallas as pl
from jax.experimental.pallas import tpu as pltpu

F32 = jnp.float32
BF16 = jnp.bfloat16

D_MODEL = 1024
N_META = 16
SSM_GROUPS = 64
SSM_GROUP = 16
SSM_STATE = 64
SSM_CHUNK = 16
SSM_FLAT = SSM_CHUNK * SSM_GROUP
SSM_STATE_LANES = 2 * SSM_STATE
HGRN_HEADS = 8
HGRN_HEAD_DIM = 128
HGRN_CHUNK = 32
HGRN_HEADS_PER_STEP = 2
D_FF = 2816
FF_CHUNK = 256
N_FF_CHUNKS = D_FF // FF_CHUNK
CONV_WIDTH = 3
IN_BLOCKS = 7
EPS = 1e-6
GELU_C0 = math.sqrt(2.0 / math.pi)
GELU_C1 = 0.044715 * GELU_C0
LANES = 128
SUBLANES = 8
N_LANE_SLABS = D_MODEL // LANES
GROUPS_PER_SLAB = LANES // SSM_GROUP
VMEM_LIMIT_BYTES = 56 * 1024 * 1024
TILES = (64, 128, 256, 128, 128)


def _rmsnorm(x, g):
    return x * lax.rsqrt(jnp.mean(x * x, axis=-1, keepdims=True) + EPS) * g


def _sigmoid(x):
    return 0.5 * jnp.tanh(0.5 * x) + 0.5


def _silu(x):
    h = 0.5 * x
    return h + h * jnp.tanh(h)


def _gelu_tanh(x):
    h = 0.5 * x
    return h + h * jnp.tanh(x * (GELU_C0 + GELU_C1 * (x * x)))


def _resident(shape):
    n = len(shape)
    return pl.BlockSpec(shape, lambda *_: (0,) * n, pipeline_mode=pl.Buffered(1))


def _params(semantics):
    return pltpu.CompilerParams(dimension_semantics=semantics, vmem_limit_bytes=VMEM_LIMIT_BYTES)


def _rows(tt, cols, k=0):
    return pl.BlockSpec((None, tt * SUBLANES, cols), lambda g, t, k=k: (g, t, k))


def _inproj_kernel(x_ref, g_ref, w_ref, p_ref, h_ref, slab_ref, *, tt):
    for b in range(SUBLANES):
        for j in range(N_LANE_SLABS):
            slab_ref[j, pl.ds(b, tt, stride=SUBLANES), :] = x_ref[b, :, j * LANES:(j + 1) * LANES]
    h = jnp.concatenate([slab_ref[j] for j in range(N_LANE_SLABS)], axis=1)
    h_ref[...] = h
    z = _rmsnorm(h, g_ref[...]).astype(BF16)
    for k in range(IN_BLOCKS):
        cols = slice(k * D_MODEL, (k + 1) * D_MODEL)
        p_ref[:, cols] = jnp.dot(z, w_ref[:, cols], preferred_element_type=F32).astype(BF16)


def _inproj(x, g, w, tt):
    b, l, _ = x.shape
    ng = b // SUBLANES
    return pl.pallas_call(
        functools.partial(_inproj_kernel, tt=tt),
        grid=(ng, l // tt),
        in_specs=[
            pl.BlockSpec((SUBLANES, tt, D_MODEL), lambda i, t: (i, t, 0)),
            _resident((1, D_MODEL)),
            _resident((D_MODEL, IN_BLOCKS * D_MODEL)),
        ],
        out_specs=[_rows(tt, IN_BLOCKS * D_MODEL), _rows(tt, D_MODEL)],
        out_shape=[
            jax.ShapeDtypeStruct((ng, l * SUBLANES, IN_BLOCKS * D_MODEL), BF16),
            jax.ShapeDtypeStruct((ng, l * SUBLANES, D_MODEL), F32),
        ],
        scratch_shapes=[pltpu.VMEM((N_LANE_SLABS, tt * SUBLANES, LANES), F32)],
        compiler_params=_params(("parallel", "parallel")),
        name="inproj",
    )(x, g, w)


def _s5_kernel(u_ref, w1_ref, ms_ref, c1_ref, c2_ref, d_ref, wglu_ref, s0_ref,
               ya_ref, sfin_ref, st_ref, z_ref, o_ref, y_ref, *, tt):
    ti = pl.program_id(1)

    @pl.when(ti == 0)
    def _():
        st_ref[...] = s0_ref[...]

    nc = tt // SSM_CHUNK
    m = nc * SUBLANES
    chunk_rows = SSM_CHUNK * SUBLANES
    half = SSM_CHUNK // 2
    lane_slot = lax.broadcasted_iota(jnp.int32, (m, LANES), 1) // SSM_GROUP
    in_slot = [lane_slot == s for s in range(GROUPS_PER_SLAB)]

    u = u_ref[...].astype(F32)

    def step_rows(x, t, j):
        return jnp.concatenate(
            [x[c * chunk_rows + t * SUBLANES:c * chunk_rows + (t + 1) * SUBLANES, j * LANES:(j + 1) * LANES]
             for c in range(nc)], axis=0)

    def slab_outputs(j):
        groups = range(j * GROUPS_PER_SLAB, (j + 1) * GROUPS_PER_SLAB)
        gs = slice(j * GROUPS_PER_SLAB, (j + 1) * GROUPS_PER_SLAB)
        rs = [jnp.dot(z_ref[g], w1_ref[g], preferred_element_type=F32) for g in groups]
        s, sw = st_ref[0, gs], st_ref[1, gs]
        c1, c2 = c1_ref[gs], c2_ref[gs]
        carried = []
        for c in range(nc):
            carried.append(s)
            rows_c = slice(c * SUBLANES, (c + 1) * SUBLANES)
            inc = jnp.stack([r[rows_c, SSM_FLAT:SSM_FLAT + SSM_STATE_LANES] for r in rs])
            inc_sw = jnp.stack([r[rows_c, SSM_FLAT + SSM_STATE_LANES:] for r in rs])
            s, sw = c1 * s + c2 * sw + inc, c1 * sw - c2 * s + inc_sw
        st_ref[0, gs] = s
        st_ref[1, gs] = sw
        for i, g in enumerate(groups):
            before = jnp.concatenate([cs[i] for cs in carried], axis=0).astype(BF16)
            o_ref[g] = rs[i][:, :SSM_FLAT] + jnp.dot(before, ms_ref[g], preferred_element_type=F32)

    for j in range(N_LANE_SLABS):
        rot = []
        for t in range(SSM_CHUNK):
            k = t % half
            x = step_rows(u, t, j)
            rot.append(x if k == 0 else pltpu.roll(x, SSM_GROUP * k, axis=1))
        for gg in range(GROUPS_PER_SLAB):
            for hf in range(2):
                acc = rot[half * hf]
                for k in range(1, half):
                    acc = jnp.where(in_slot[(gg + k) % half], rot[half * hf + k], acc)
                z_ref[j * GROUPS_PER_SLAB + gg, :, hf * LANES:(hf + 1) * LANES] = acc.astype(BF16)

    for j in range(N_LANE_SLABS):
        slab_outputs(j)

    for j in range(N_LANE_SLABS):
        for hf in range(2):
            ys = [o_ref[j * GROUPS_PER_SLAB + gg, :, hf * LANES:(hf + 1) * LANES]
                  for gg in range(GROUPS_PER_SLAB)]
            for k in range(half):
                acc = ys[0]
                for gg in range(1, GROUPS_PER_SLAB):
                    acc = jnp.where(in_slot[(gg + k) % half], ys[gg], acc)
                nat = acc if k == 0 else pltpu.roll(acc, LANES - SSM_GROUP * k, axis=1)
                t = half * hf + k
                for c in range(nc):
                    y_ref[c * chunk_rows + t * SUBLANES:c * chunk_rows + (t + 1) * SUBLANES,
                          j * LANES:(j + 1) * LANES] = nat[c * SUBLANES:(c + 1) * SUBLANES]

    y = _gelu_tanh(y_ref[...] + d_ref[...] * u)
    gate = jnp.dot(y.astype(BF16), wglu_ref[...], preferred_element_type=F32)
    ya_ref[...] = (y * _sigmoid(gate)).astype(BF16)

    @pl.when(ti == pl.num_programs(1) - 1)
    def _():
        sfin_ref[...] = st_ref[...]


def _s5(p, w1, ms, c1, c2, d_skip, w_glu, s0, tt):
    ng, rows, _ = p.shape
    m = tt // SSM_CHUNK * SUBLANES
    state_shape = (2, SSM_GROUPS, SUBLANES, SSM_STATE_LANES)
    return pl.pallas_call(
        functools.partial(_s5_kernel, tt=tt),
        grid=(ng, rows // (tt * SUBLANES)),
        in_specs=[
            _rows(tt, D_MODEL, 0),
            _resident(w1.shape),
            _resident(ms.shape),
            _resident(c1.shape),
            _resident(c2.shape),
            _resident((1, D_MODEL)),
            _resident((D_MODEL, D_MODEL)),
            pl.BlockSpec((None,) + state_shape, lambda g, t: (0, 0, 0, 0, 0)),
        ],
        out_specs=[
            _rows(tt, D_MODEL),
            pl.BlockSpec((None,) + state_shape, lambda g, t: (g, 0, 0, 0, 0)),
        ],
        out_shape=[
            jax.ShapeDtypeStruct((ng, rows, D_MODEL), BF16),
            jax.ShapeDtypeStruct((ng,) + state_shape, F32),
        ],
        scratch_shapes=[
            pltpu.VMEM(state_shape, F32),
            pltpu.VMEM((SSM_GROUPS, m, SSM_FLAT), BF16),
            pltpu.VMEM((SSM_GROUPS, m, SSM_FLAT), F32),
            pltpu.VMEM((tt * SUBLANES, D_MODEL), F32),
        ],
        compiler_params=_params(("parallel", "arbitrary")),
        name="s5",
    )(p, w1, ms, c1, c2, d_skip, w_glu, s0)


def _hgrn_kernel(q_ref, f_ref, i_ref, og_ref, lb_ref, ng_ref, s0_ref,
                 yb_ref, sfin_ref, st_ref, qin_ref, kout_ref, v_ref, oint_ref, *, tt, chunk, heads):
    ti = pl.program_id(2)

    @pl.when(ti == 0)
    def _():
        st_ref[...] = s0_ref[...]

    rows = chunk * SUBLANES
    r_idx = lax.broadcasted_iota(jnp.int32, (rows, rows), 0)
    c_idx = lax.broadcasted_iota(jnp.int32, (rows, rows), 1)
    allowed = (((r_idx - c_idx) & (SUBLANES - 1)) == 0) & (c_idx <= r_idx)
    mid = chunk // 2 - 1
    head = (chunk, SUBLANES, HGRN_HEAD_DIM)

    def decays(hh, ci):
        r = slice(ci * rows, (ci + 1) * rows)
        slot = ci % 2
        lanes = slice(hh * HGRN_HEAD_DIM, (hh + 1) * HGRN_HEAD_DIM)
        half_gap = 0.5 * (1.0 - lb_ref[:, lanes])
        swing = half_gap * jnp.tanh(0.5 * f_ref[r, lanes].astype(F32))
        f = (1.0 - half_gap) + swing
        k = (half_gap - swing).reshape(head)
        cum = jnp.log(f).reshape(head)
        shift = 1
        while shift < chunk:
            cum = cum + jnp.concatenate([jnp.zeros((shift,) + head[1:], F32), cum[:-shift]], axis=0)
            shift *= 2
        ref_row = cum[mid:mid + 1]
        last = cum[chunk - 1:chunk]
        q = q_ref[r, lanes].astype(F32).reshape(head)
        q_mid = q * jnp.exp(cum - ref_row)
        k_mid = k * jnp.exp(ref_row - cum)
        qin_ref[slot, hh] = (q_mid * jnp.exp(ref_row)).reshape(rows, HGRN_HEAD_DIM)
        kout_ref[slot, hh] = (k_mid * jnp.exp(last - ref_row)).reshape(rows, HGRN_HEAD_DIM)
        v = i_ref[r, lanes]
        v_ref[slot, hh] = v.astype(F32)
        return (q_mid.reshape(rows, HGRN_HEAD_DIM).astype(BF16),
                k_mid.reshape(rows, HGRN_HEAD_DIM).astype(BF16), v, jnp.exp(last[0]))

    pairs = [(hh, b) for hh in range(heads) for b in range(SUBLANES)]
    rb = [pl.ds(b, chunk, stride=SUBLANES) for b in range(SUBLANES)]

    def matmuls(ci, operands):
        r = slice(ci * rows, (ci + 1) * rows)
        slot = ci % 2
        q_mid, k_mid, v, decay = zip(*operands)
        scores = [lax.dot_general(q_mid[hh], k_mid[hh], (((1,), (1,)), ((), ())),
                                  preferred_element_type=F32) for hh in range(heads)]
        scores = [jnp.where(allowed, s, 0.0).astype(BF16) for s in scores]
        o = [jnp.dot(scores[hh], v[hh], preferred_element_type=F32) for hh in range(heads)]
        for hh, b in pairs:
            state_t = st_ref[hh, b]
            oint_ref[slot, hh, rb[b], :] = lax.dot_general(
                qin_ref[slot, hh, rb[b], :].astype(BF16), state_t.astype(BF16),
                (((1,), (1,)), ((), ())), preferred_element_type=F32)
            st_ref[hh, b] = state_t * decay[hh][b:b + 1, :] + lax.dot_general(
                v_ref[slot, hh, rb[b], :].astype(BF16), kout_ref[slot, hh, rb[b], :].astype(BF16),
                (((0,), (0,)), ((), ())), preferred_element_type=F32)
        for hh in range(heads):
            lanes = slice(hh * HGRN_HEAD_DIM, (hh + 1) * HGRN_HEAD_DIM)
            out = _rmsnorm(o[hh] + oint_ref[slot, hh], ng_ref[:, lanes])
            yb_ref[r, lanes] = (out * _silu(og_ref[r, lanes].astype(F32))).astype(BF16)

    n_chunks = tt // chunk
    operands = [decays(hh, 0) for hh in range(heads)]
    for ci in range(n_chunks):
        following = [decays(hh, ci + 1) for hh in range(heads)] if ci + 1 < n_chunks else None
        matmuls(ci, operands)
        operands = following

    @pl.when(ti == pl.num_programs(2) - 1)
    def _():
        sfin_ref[...] = st_ref[...]


def _hgrn(p, lb, ng_, s0, tt, chunk):
    ng, rows, _ = p.shape
    heads = HGRN_HEADS_PER_STEP
    width = heads * HGRN_HEAD_DIM
    steps = HGRN_HEADS // heads
    col = lambda k: pl.BlockSpec((None, tt * SUBLANES, width), lambda g, h, t, k=k: (g, t, k * steps + h))
    per_head = pl.BlockSpec((1, width), lambda g, h, t: (0, h))
    state_shape = (heads, SUBLANES, HGRN_HEAD_DIM, HGRN_HEAD_DIM)
    chunk_buf = pltpu.VMEM((2, heads, chunk * SUBLANES, HGRN_HEAD_DIM), F32)
    return pl.pallas_call(
        functools.partial(_hgrn_kernel, tt=tt, chunk=chunk, heads=heads),
        grid=(ng, steps, rows // (tt * SUBLANES)),
        in_specs=[col(1), col(2), col(3), col(4), per_head, per_head,
                  pl.BlockSpec((None,) + state_shape, lambda g, h, t: (0, h, 0, 0, 0))],
        out_specs=[
            pl.BlockSpec((None, tt * SUBLANES, width), lambda g, h, t: (g, t, h)),
            pl.BlockSpec((None,) + state_shape, lambda g, h, t: (g, h, 0, 0, 0)),
        ],
        out_shape=[
            jax.ShapeDtypeStruct((ng, rows, D_MODEL), BF16),
            jax.ShapeDtypeStruct((ng, HGRN_HEADS, SUBLANES, HGRN_HEAD_DIM, HGRN_HEAD_DIM), F32),
        ],
        scratch_shapes=[pltpu.VMEM(state_shape, F32), chunk_buf, chunk_buf, chunk_buf, chunk_buf],
        compiler_params=_params(("parallel", "parallel", "arbitrary")),
        name="hgrn2",
    )(p, p, p, p, lb, ng_, s0)


def _merge_kernel(h_ref, ga_ref, gb_ref, ya_ref, yb_ref, wa_ref, wb_ref, wo_ref, o_ref):
    a = jnp.dot(ya_ref[...], wa_ref[...], preferred_element_type=F32)
    b = jnp.dot(yb_ref[...], wb_ref[...], preferred_element_type=F32)
    merged = _sigmoid(ga_ref[...].astype(F32)) * a + _sigmoid(gb_ref[...].astype(F32)) * b
    o_ref[...] = h_ref[...] + jnp.dot(merged.astype(BF16), wo_ref[...], preferred_element_type=F32)


def _merge(h, p, ya, yb, wa, wb, wo, tt):
    ng, rows, _ = h.shape
    w = _resident((D_MODEL, D_MODEL))
    return pl.pallas_call(
        _merge_kernel,
        grid=(ng, rows // (tt * SUBLANES)),
        in_specs=[_rows(tt, D_MODEL), _rows(tt, D_MODEL, 5), _rows(tt, D_MODEL, 6),
                  _rows(tt, D_MODEL), _rows(tt, D_MODEL), w, w, w],
        out_specs=_rows(tt, D_MODEL),
        out_shape=jax.ShapeDtypeStruct((ng, rows, D_MODEL), F32),
        compiler_params=_params(("parallel", "parallel")),
        name="merge",
    )(h, p, p, ya, yb, wa, wb, wo)


def _ffn_kernel(h_ref, g_ref, wu_ref, cw_ref, cb_ref, wd_ref, gf_ref,
                c0_ref, o_ref, cfin_ref, carry_ref, act_ref, slab_ref, *, tt):
    ti = pl.program_id(1)

    @pl.when(ti == 0)
    def _():
        carry_ref[...] = c0_ref[...]

    rows = tt * SUBLANES
    tail = (CONV_WIDTH - 1) * SUBLANES
    h = h_ref[...]
    z = _rmsnorm(h, g_ref[...]).astype(BF16)

    def conv_up(n):
        cols = slice(n * FF_CHUNK, (n + 1) * FF_CHUNK)
        pre = jnp.dot(z, wu_ref[:, cols], preferred_element_type=F32)
        ext = jnp.concatenate([carry_ref[n], pre], axis=0)
        carry_ref[n] = ext[rows:rows + tail]
        out = cb_ref[:, cols]
        for j in range(CONV_WIDTH):
            out = out + ext[j * SUBLANES:j * SUBLANES + rows] * cw_ref[j:j + 1, cols]
        return out

    for n in range(N_FF_CHUNKS):
        gate = conv_up(n)
        val = conv_up(N_FF_CHUNKS + n)
        act_ref[:, n * FF_CHUNK:(n + 1) * FF_CHUNK] = (_silu(gate) * val).astype(BF16)

    down = jnp.dot(act_ref[...], wd_ref[...], preferred_element_type=F32)
    out = _rmsnorm(h + down, gf_ref[...])
    for j in range(N_LANE_SLABS):
        slab_ref[j] = out[:, j * LANES:(j + 1) * LANES]
    for b in range(SUBLANES):
        for j in range(N_LANE_SLABS):
            o_ref[b, :, j * LANES:(j + 1) * LANES] = slab_ref[j, pl.ds(b, tt, stride=SUBLANES), :]

    @pl.when(ti == pl.num_programs(1) - 1)
    def _():
        cfin_ref[...] = carry_ref[...]


def _ffn(h, g, wu, cw, cb, wd, gf, c0, tt):
    ng, rows, _ = h.shape
    l = rows // SUBLANES
    carry_shape = (2 * N_FF_CHUNKS, (CONV_WIDTH - 1) * SUBLANES, FF_CHUNK)
    return pl.pallas_call(
        functools.partial(_ffn_kernel, tt=tt),
        grid=(ng, l // tt),
        in_specs=[
            _rows(tt, D_MODEL),
            _resident((1, D_MODEL)),
            _resident(wu.shape), _resident(cw.shape), _resident(cb.shape), _resident(wd.shape),
            _resident((1, D_MODEL)),
            pl.BlockSpec((None,) + carry_shape, lambda g, t: (0, 0, 0, 0)),
        ],
        out_specs=[
            pl.BlockSpec((SUBLANES, tt, D_MODEL), lambda g, t: (g, t, 0)),
            pl.BlockSpec((None,) + carry_shape, lambda g, t: (g, 0, 0, 0)),
        ],
        out_shape=[
            jax.ShapeDtypeStruct((ng * SUBLANES, l, D_MODEL), F32),
            jax.ShapeDtypeStruct((ng,) + carry_shape, F32),
        ],
        scratch_shapes=[
            pltpu.VMEM(carry_shape, F32),
            pltpu.VMEM((tt * SUBLANES, D_FF), BF16),
            pltpu.VMEM((N_LANE_SLABS, tt * SUBLANES, LANES), F32),
        ],
        compiler_params=_params(("parallel", "arbitrary")),
        name="ffn",
    )(h, g, wu, cw, cb, wd, gf, c0)


def _s5_params(lam_re, lam_im, log_dt, b_re, b_im, c_re, c_im):
    hp = lax.Precision.HIGHEST
    t_len = SSM_CHUNK
    dt = jnp.exp(log_dt)[:, None]
    mag = jnp.exp(lam_re * dt)
    ab_re = mag * jnp.cos(lam_im * dt)
    ab_im = mag * jnp.sin(lam_im * dt)
    den = lam_re * lam_re + lam_im * lam_im
    nr = ab_re - 1.0
    coef_re = (nr * lam_re + ab_im * lam_im) / den
    coef_im = (ab_im * lam_re - nr * lam_im) / den
    bb_re = coef_re[..., None] * b_re - coef_im[..., None] * b_im
    bb_im = coef_re[..., None] * b_im + coef_im[..., None] * b_re
    tau = jnp.arange(t_len + 1, dtype=F32)[None, :, None]
    pw_mag = jnp.exp((lam_re * dt)[:, None, :] * tau)
    pw_re = pw_mag * jnp.cos((lam_im * dt)[:, None, :] * tau)
    pw_im = pw_mag * jnp.sin((lam_im * dt)[:, None, :] * tau)
    ab_b_re = pw_re[..., None] * bb_re[:, None] - pw_im[..., None] * bb_im[:, None]
    ab_b_im = pw_re[..., None] * bb_im[:, None] + pw_im[..., None] * bb_re[:, None]
    kern = (jnp.einsum("gop,gtpi->gtoi", c_re, ab_b_re[:, :t_len], precision=hp)
            - jnp.einsum("gop,gtpi->gtoi", c_im, ab_b_im[:, :t_len], precision=hp))
    out_re = c_re[:, None] * pw_re[:, :, None, :] - c_im[:, None] * pw_im[:, :, None, :]
    out_im = -(c_re[:, None] * pw_im[:, :, None, :] + c_im[:, None] * pw_re[:, :, None, :])

    half = t_len // 2
    slots = np.arange(t_len)
    step_of = np.stack([half * (slots // half) + (slots % half - gg) % half
                        for gg in range(GROUPS_PER_SLAB)])
    lags = np.arange(t_len + 1)
    lag_sel = (step_of[:, None, None, :] - step_of[:, None, :, None]) == lags[None, :t_len, None, None]
    inc_sel = (t_len - 1 - step_of[:, :, None]) == lags[None, None, :t_len]
    out_sel = (step_of[:, :, None] + 1) == lags[None, None, :]
    by_slab = lambda a: a.astype(BF16).reshape((N_LANE_SLABS, GROUPS_PER_SLAB) + a.shape[1:])
    select = lambda spec, sel, a: jnp.einsum(spec, jnp.asarray(sel, BF16), by_slab(a),
                                             preferred_element_type=F32).astype(BF16)
    lag_eye = (jnp.asarray(lag_sel, BF16)[:, :, None, :, :, None]
               * jnp.eye(SSM_GROUP, dtype=BF16)[None, None, :, None, None, :])
    lag_eye = lag_eye.reshape(GROUPS_PER_SLAB, SSM_FLAT, t_len, SSM_FLAT)
    kern_rows = by_slab(kern.transpose(0, 3, 1, 2).reshape(SSM_GROUPS, SSM_GROUP, SSM_FLAT))
    toep = jnp.einsum("jxik,xkrC->jxriC", kern_rows, lag_eye, preferred_element_type=F32)
    toep = toep.astype(BF16).reshape(SSM_GROUPS, SSM_FLAT, SSM_FLAT)
    inc_re = select("xsu,jxuph->jxshp", inc_sel, ab_b_re[:, :t_len])
    inc_im = select("xsu,jxuph->jxshp", inc_sel, ab_b_im[:, :t_len])
    inc = jnp.concatenate([inc_re, inc_im, inc_im, inc_re], axis=-1)
    w1 = jnp.concatenate([toep, inc.reshape(SSM_GROUPS, SSM_FLAT, 2 * SSM_STATE_LANES)], axis=-1)
    ms = jnp.concatenate([select("xsu,jxuop->jxpso", out_sel, out_re),
                          select("xsu,jxuop->jxpso", out_sel, out_im)], axis=2)
    ms = ms.reshape(SSM_GROUPS, SSM_STATE_LANES, SSM_FLAT)
    a16_re, a16_im = pw_re[:, t_len], pw_im[:, t_len]
    c1 = jnp.concatenate([a16_re, a16_re], axis=-1)
    c2 = jnp.concatenate([-a16_im, a16_im], axis=-1)
    bcast = lambda a: jnp.broadcast_to(a[:, None, :], (SSM_GROUPS, SUBLANES, SSM_STATE_LANES))
    return w1, ms, bcast(c1), bcast(c2)


def _block(x, states, w):
    l = x.shape[1]
    s5_state, hgrn_state, conv_carry = states
    t_in, t_s5, t_hgrn, t_merge, t_ffn = (min(t, l) for t in TILES)
    p, h = _inproj(x, w["mix_g"], w["w_in"], t_in)
    ya, s5_state = _s5(p, w["s5_w1"], w["s5_ms"], w["s5_c1"], w["s5_c2"], w["d_skip"], w["w_glu"],
                       s5_state, t_s5)
    yb, hgrn_state = _hgrn(p, w["lb"], w["hgrn_g"], hgrn_state, t_hgrn, min(HGRN_CHUNK, l))
    h1 = _merge(h, p, ya, yb, w["w_a"], w["w_b"], w["w_out"], t_merge)
    out, conv_carry = _ffn(h1, w["ffn_g"], w["w_up"], w["conv_w"], w["conv_b"], w["w_down"],
                           w["final_g"], conv_carry, t_ffn)
    return out, (s5_state, hgrn_state, conv_carry)


def kernel(x, meta_tokens, mix_norm_g, w_in, ssm_lambda_re, ssm_lambda_im, ssm_log_dt, ssm_b_re, ssm_b_im, ssm_c_re, ssm_c_im, ssm_d, ssm_w_glu, w_ssm_proj, hgrn_lb_logits, hgrn_norm_g, w_hgrn_proj, w_out, ffn_norm_g, w_up, conv_w, conv_b, w_down, final_norm_g):
    assert x.shape[0] % SUBLANES == 0 and x.shape[1] % SSM_CHUNK == 0 and x.shape[2] == D_MODEL
    s5_w1, s5_ms, s5_c1, s5_c2 = _s5_params(ssm_lambda_re[0], ssm_lambda_im[0], ssm_log_dt[0],
                                            ssm_b_re[0], ssm_b_im[0], ssm_c_re[0], ssm_c_im[0])
    lower_bounds = jnp.cumsum(jax.nn.softmax(hgrn_lb_logits.astype(F32), axis=0), axis=0)
    w = {
        "mix_g": mix_norm_g[0].reshape(1, D_MODEL),
        "w_in": w_in[0].astype(BF16),
        "s5_w1": s5_w1, "s5_ms": s5_ms, "s5_c1": s5_c1, "s5_c2": s5_c2,
        "d_skip": ssm_d[0].reshape(1, D_MODEL),
        "w_glu": ssm_w_glu[0].astype(BF16),
        "lb": lower_bounds[0].reshape(1, D_MODEL),
        "hgrn_g": jnp.tile(hgrn_norm_g[0], HGRN_HEADS).reshape(1, D_MODEL),
        "w_a": w_ssm_proj[0].astype(BF16),
        "w_b": w_hgrn_proj[0].astype(BF16),
        "w_out": w_out[0].astype(BF16),
        "ffn_g": ffn_norm_g[0].reshape(1, D_MODEL),
        "w_up": w_up[0].astype(BF16),
        "conv_w": conv_w[0],
        "conv_b": conv_b[0].reshape(1, 2 * D_FF),
        "w_down": w_down[0].astype(BF16),
        "final_g": final_norm_g.reshape(1, D_MODEL),
    }
    zero_states = (
        jnp.zeros((1, 2, SSM_GROUPS, SUBLANES, SSM_STATE_LANES), F32),
        jnp.zeros((1, HGRN_HEADS, SUBLANES, HGRN_HEAD_DIM, HGRN_HEAD_DIM), F32),
        jnp.zeros((1, 2 * N_FF_CHUNKS, (CONV_WIDTH - 1) * SUBLANES, FF_CHUNK), F32),
    )
    meta = jnp.broadcast_to(meta_tokens.astype(x.dtype)[None], (SUBLANES, N_META, D_MODEL))
    _, states = _block(meta, zero_states, w)
    out, _ = _block(x, states, w)
    return out
```

```python
import functools
import math

import numpy as np
import jax
import jax.numpy as jnp
from jax import lax
from jax.experimental import pallas as pl
from jax.experimental.pallas import tpu as pltpu

F32 = jnp.float32
BF16 = jnp.bfloat16

D_MODEL = 1024
N_META = 16
SSM_GROUPS = 64
SSM_GROUP = 16
SSM_STATE = 64
SSM_CHUNK = 16
SSM_FLAT = SSM_CHUNK * SSM_GROUP
SSM_STATE_LANES = 2 * SSM_STATE
HGRN_HEADS = 8
HGRN_HEAD_DIM = 128
HGRN_CHUNK = 32
HGRN_HEADS_PER_STEP = 2
D_FF = 2816
FF_CHUNK = 256
N_FF_CHUNKS = D_FF // FF_CHUNK
CONV_WIDTH = 3
IN_BLOCKS = 7
EPS = 1e-6
GELU_C0 = math.sqrt(2.0 / math.pi)
GELU_C1 = 0.044715 * GELU_C0
LANES = 128
SUBLANES = 8
N_LANE_SLABS = D_MODEL // LANES
GROUPS_PER_SLAB = LANES // SSM_GROUP
VMEM_LIMIT_BYTES = 56 * 1024 * 1024
TILES = (64, 128, 256, 128, 128)


def _rmsnorm(x, g):
    return x * lax.rsqrt(jnp.mean(x * x, axis=-1, keepdims=True) + EPS) * g


def _sigmoid(x):
    return 0.5 * jnp.tanh(0.5 * x) + 0.5


def _silu(x):
    h = 0.5 * x
    return h + h * jnp.tanh(h)


def _gelu_tanh(x):
    h = 0.5 * x
    return h + h * jnp.tanh(x * (GELU_C0 + GELU_C1 * (x * x)))


def _resident(shape):
    n = len(shape)
    return pl.BlockSpec(shape, lambda *_: (0,) * n, pipeline_mode=pl.Buffered(1))


def _params(semantics):
    return pltpu.CompilerParams(dimension_semantics=semantics, vmem_limit_bytes=VMEM_LIMIT_BYTES)


def _rows(tt, cols, k=0):
    return pl.BlockSpec((None, tt * SUBLANES, cols), lambda g, t, k=k: (g, t, k))


def _inproj_kernel(x_ref, g_ref, w_ref, p_ref, h_ref, slab_ref, *, tt):
    for b in range(SUBLANES):
        for j in range(N_LANE_SLABS):
            slab_ref[j, pl.ds(b, tt, stride=SUBLANES), :] = x_ref[b, :, j * LANES:(j + 1) * LANES]
    h = jnp.concatenate([slab_ref[j] for j in range(N_LANE_SLABS)], axis=1)
    h_ref[...] = h
    z = _rmsnorm(h, g_ref[...]).astype(BF16)
    for k in range(IN_BLOCKS):
        cols = slice(k * D_MODEL, (k + 1) * D_MODEL)
        p_ref[:, cols] = jnp.dot(z, w_ref[:, cols], preferred_element_type=F32).astype(BF16)


def _inproj(x, g, w, tt):
    b, l, _ = x.shape
    ng = b // SUBLANES
    return pl.pallas_call(
        functools.partial(_inproj_kernel, tt=tt),
        grid=(ng, l // tt),
        in_specs=[
            pl.BlockSpec((SUBLANES, tt, D_MODEL), lambda i, t: (i, t, 0)),
            _resident((1, D_MODEL)),
            _resident((D_MODEL, IN_BLOCKS * D_MODEL)),
        ],
        out_specs=[_rows(tt, IN_BLOCKS * D_MODEL), _rows(tt, D_MODEL)],
        out_shape=[
            jax.ShapeDtypeStruct((ng, l * SUBLANES, IN_BLOCKS * D_MODEL), BF16),
            jax.ShapeDtypeStruct((ng, l * SUBLANES, D_MODEL), F32),
        ],
        scratch_shapes=[pltpu.VMEM((N_LANE_SLABS, tt * SUBLANES, LANES), F32)],
        compiler_params=_params(("parallel", "parallel")),
        name="inproj",
    )(x, g, w)


def _s5_kernel(u_ref, w1_ref, ms_ref, c1_ref, c2_ref, d_ref, wglu_ref, s0_ref,
               ya_ref, sfin_ref, st_ref, z_ref, o_ref, y_ref, *, tt):
    ti = pl.program_id(1)

    @pl.when(ti == 0)
    def _():
        st_ref[...] = s0_ref[...]

    nc = tt // SSM_CHUNK
    m = nc * SUBLANES
    chunk_rows = SSM_CHUNK * SUBLANES
    half = SSM_CHUNK // 2
    lane_slot = lax.broadcasted_iota(jnp.int32, (m, LANES), 1) // SSM_GROUP
    in_slot = [lane_slot == s for s in range(GROUPS_PER_SLAB)]

    u = u_ref[...].astype(F32)

    def step_rows(x, t, j):
        return jnp.concatenate(
            [x[c * chunk_rows + t * SUBLANES:c * chunk_rows + (t + 1) * SUBLANES, j * LANES:(j + 1) * LANES]
             for c in range(nc)], axis=0)

    def slab_outputs(j):
        groups = range(j * GROUPS_PER_SLAB, (j + 1) * GROUPS_PER_SLAB)
        gs = slice(j * GROUPS_PER_SLAB, (j + 1) * GROUPS_PER_SLAB)
        rs = [jnp.dot(z_ref[g], w1_ref[g], preferred_element_type=F32) for g in groups]
        s, sw = st_ref[0, gs], st_ref[1, gs]
        c1, c2 = c1_ref[gs], c2_ref[gs]
        carried = []
        for c in range(nc):
            carried.append(s)
            rows_c = slice(c * SUBLANES, (c + 1) * SUBLANES)
            inc = jnp.stack([r[rows_c, SSM_FLAT:SSM_FLAT + SSM_STATE_LANES] for r in rs])
            inc_sw = jnp.stack([r[rows_c, SSM_FLAT + SSM_STATE_LANES:] for r in rs])
            s, sw = c1 * s + c2 * sw + inc, c1 * sw - c2 * s + inc_sw
        st_ref[0, gs] = s
        st_ref[1, gs] = sw
        for i, g in enumerate(groups):
            before = jnp.concatenate([cs[i] for cs in carried], axis=0).astype(BF16)
            o_ref[g] = rs[i][:, :SSM_FLAT] + jnp.dot(before, ms_ref[g], preferred_element_type=F32)

    for j in range(N_LANE_SLABS):
        rot = []
        for t in range(SSM_CHUNK):
            k = t % half
            x = step_rows(u, t, j)
            rot.append(x if k == 0 else pltpu.roll(x, SSM_GROUP * k, axis=1))
        for gg in range(GROUPS_PER_SLAB):
            for hf in range(2):
                acc = rot[half * hf]
                for k in range(1, half):
                    acc = jnp.where(in_slot[(gg + k) % half], rot[half * hf + k], acc)
                z_ref[j * GROUPS_PER_SLAB + gg, :, hf * LANES:(hf + 1) * LANES] = acc.astype(BF16)

    for j in range(N_LANE_SLABS):
        slab_outputs(j)

    for j in range(N_LANE_SLABS):
        for hf in range(2):
            ys = [o_ref[j * GROUPS_PER_SLAB + gg, :, hf * LANES:(hf + 1) * LANES]
                  for gg in range(GROUPS_PER_SLAB)]
            for k in range(half):
                acc = ys[0]
                for gg in range(1, GROUPS_PER_SLAB):
                    acc = jnp.where(in_slot[(gg + k) % half], ys[gg], acc)
                nat = acc if k == 0 else pltpu.roll(acc, LANES - SSM_GROUP * k, axis=1)
                t = half * hf + k
                for c in range(nc):
                    y_ref[c * chunk_rows + t * SUBLANES:c * chunk_rows + (t + 1) * SUBLANES,
                          j * LANES:(j + 1) * LANES] = nat[c * SUBLANES:(c + 1) * SUBLANES]

    y = _gelu_tanh(y_ref[...] + d_ref[...] * u)
    gate = jnp.dot(y.astype(BF16), wglu_ref[...], preferred_element_type=F32)
    ya_ref[...] = (y * _sigmoid(gate)).astype(BF16)

    @pl.when(ti == pl.num_programs(1) - 1)
    def _():
        sfin_ref[...] = st_ref[...]


def _s5(p, w1, ms, c1, c2, d_skip, w_glu, s0, tt):
    ng, rows, _ = p.shape
    m = tt // SSM_CHUNK * SUBLANES
    state_shape = (2, SSM_GROUPS, SUBLANES, SSM_STATE_LANES)
    return pl.pallas_call(
        functools.partial(_s5_kernel, tt=tt),
        grid=(ng, rows // (tt * SUBLANES)),
        in_specs=[
            _rows(tt, D_MODEL, 0),
            _resident(w1.shape),
            _resident(ms.shape),
            _resident(c1.shape),
            _resident(c2.shape),
            _resident((1, D_MODEL)),
            _resident((D_MODEL, D_MODEL)),
            pl.BlockSpec((None,) + state_shape, lambda g, t: (0, 0, 0, 0, 0)),
        ],
        out_specs=[
            _rows(tt, D_MODEL),
            pl.BlockSpec((None,) + state_shape, lambda g, t: (g, 0, 0, 0, 0)),
        ],
        out_shape=[
            jax.ShapeDtypeStruct((ng, rows, D_MODEL), BF16),
            jax.ShapeDtypeStruct((ng,) + state_shape, F32),
        ],
        scratch_shapes=[
            pltpu.VMEM(state_shape, F32),
            pltpu.VMEM((SSM_GROUPS, m, SSM_FLAT), BF16),
            pltpu.VMEM((SSM_GROUPS, m, SSM_FLAT), F32),
            pltpu.VMEM((tt * SUBLANES, D_MODEL), F32),
        ],
        compiler_params=_params(("parallel", "arbitrary")),
        name="s5",
    )(p, w1, ms, c1, c2, d_skip, w_glu, s0)


def _hgrn_kernel(q_ref, f_ref, i_ref, og_ref, lb_ref, ng_ref, s0_ref,
                 yb_ref, sfin_ref, st_ref, qin_ref, kout_ref, v_ref, oint_ref, *, tt, chunk, heads):
    ti = pl.program_id(2)

    @pl.when(ti == 0)
    def _():
        st_ref[...] = s0_ref[...]

    rows = chunk * SUBLANES
    r_idx = lax.broadcasted_iota(jnp.int32, (rows, rows), 0)
    c_idx = lax.broadcasted_iota(jnp.int32, (rows, rows), 1)
    allowed = (((r_idx - c_idx) & (SUBLANES - 1)) == 0) & (c_idx <= r_idx)
    mid = chunk // 2 - 1
    head = (chunk, SUBLANES, HGRN_HEAD_DIM)

    def decays(hh, ci):
        r = slice(ci * rows, (ci + 1) * rows)
        slot = ci % 2
        lanes = slice(hh * HGRN_HEAD_DIM, (hh + 1) * HGRN_HEAD_DIM)
        half_gap = 0.5 * (1.0 - lb_ref[:, lanes])
        swing = half_gap * jnp.tanh(0.5 * f_ref[r, lanes].astype(F32))
        f = (1.0 - half_gap) + swing
        k = (half_gap - swing).reshape(head)
        cum = jnp.log(f).reshape(head)
        shift = 1
        while shift < chunk:
            cum = cum + jnp.concatenate([jnp.zeros((shift,) + head[1:], F32), cum[:-shift]], axis=0)
            shift *= 2
        ref_row = cum[mid:mid + 1]
        last = cum[chunk - 1:chunk]
        q = q_ref[r, lanes].astype(F32).reshape(head)
        q_mid = q * jnp.exp(cum - ref_row)
        k_mid = k * jnp.exp(ref_row - cum)
        qin_ref[slot, hh] = (q_mid * jnp.exp(ref_row)).reshape(rows, HGRN_HEAD_DIM)
        kout_ref[slot, hh] = (k_mid * jnp.exp(last - ref_row)).reshape(rows, HGRN_HEAD_DIM)
        v = i_ref[r, lanes]
        v_ref[slot, hh] = v.astype(F32)
        return (q_mid.reshape(rows, HGRN_HEAD_DIM).astype(BF16),
                k_mid.reshape(rows, HGRN_HEAD_DIM).astype(BF16), v, jnp.exp(last[0]))

    pairs = [(hh, b) for hh in range(heads) for b in range(SUBLANES)]
    rb = [pl.ds(b, chunk, stride=SUBLANES) for b in range(SUBLANES)]

    def matmuls(ci, operands):
        r = slice(ci * rows, (ci + 1) * rows)
        slot = ci % 2
        q_mid, k_mid, v, decay = zip(*operands)
        scores = [lax.dot_general(q_mid[hh], k_mid[hh], (((1,), (1,)), ((), ())),
                                  preferred_element_type=F32) for hh in range(heads)]
        scores = [jnp.where(allowed, s, 0.0).astype(BF16) for s in scores]
        o = [jnp.dot(scores[hh], v[hh], preferred_element_type=F32) for hh in range(heads)]
        for hh, b in pairs:
            state_t = st_ref[hh, b]
            oint_ref[slot, hh, rb[b], :] = lax.dot_general(
                qin_ref[slot, hh, rb[b], :].astype(BF16), state_t.astype(BF16),
                (((1,), (1,)), ((), ())), preferred_element_type=F32)
            st_ref[hh, b] = state_t * decay[hh][b:b + 1, :] + lax.dot_general(
                v_ref[slot, hh, rb[b], :].astype(BF16), kout_ref[slot, hh, rb[b], :].astype(BF16),
                (((0,), (0,)), ((), ())), preferred_element_type=F32)
        for hh in range(heads):
            lanes = slice(hh * HGRN_HEAD_DIM, (hh + 1) * HGRN_HEAD_DIM)
            out = _rmsnorm(o[hh] + oint_ref[slot, hh], ng_ref[:, lanes])
            yb_ref[r, lanes] = (out * _silu(og_ref[r, lanes].astype(F32))).astype(BF16)

    n_chunks = tt // chunk
    operands = [decays(hh, 0) for hh in range(heads)]
    for ci in range(n_chunks):
        following = [decays(hh, ci + 1) for hh in range(heads)] if ci + 1 < n_chunks else None
        matmuls(ci, operands)
        operands = following

    @pl.when(ti == pl.num_programs(2) - 1)
    def _():
        sfin_ref[...] = st_ref[...]


def _hgrn(p, lb, ng_, s0, tt, chunk):
    ng, rows, _ = p.shape
    heads = HGRN_HEADS_PER_STEP
    width = heads * HGRN_HEAD_DIM
    steps = HGRN_HEADS // heads
    col = lambda k: pl.BlockSpec((None, tt * SUBLANES, width), lambda g, h, t, k=k: (g, t, k * steps + h))
    per_head = pl.BlockSpec((1, width), lambda g, h, t: (0, h))
    state_shape = (heads, SUBLANES, HGRN_HEAD_DIM, HGRN_HEAD_DIM)
    chunk_buf = pltpu.VMEM((2, heads, chunk * SUBLANES, HGRN_HEAD_DIM), F32)
    return pl.pallas_call(
        functools.partial(_hgrn_kernel, tt=tt, chunk=chunk, heads=heads),
        grid=(ng, steps, rows // (tt * SUBLANES)),
        in_specs=[col(1), col(2), col(3), col(4), per_head, per_head,
                  pl.BlockSpec((None,) + state_shape, lambda g, h, t: (0, h, 0, 0, 0))],
        out_specs=[
            pl.BlockSpec((None, tt * SUBLANES, width), lambda g, h, t: (g, t, h)),
            pl.BlockSpec((None,) + state_shape, lambda g, h, t: (g, h, 0, 0, 0)),
        ],
        out_shape=[
            jax.ShapeDtypeStruct((ng, rows, D_MODEL), BF16),
            jax.ShapeDtypeStruct((ng, HGRN_HEADS, SUBLANES, HGRN_HEAD_DIM, HGRN_HEAD_DIM), F32),
        ],
        scratch_shapes=[pltpu.VMEM(state_shape, F32), chunk_buf, chunk_buf, chunk_buf, chunk_buf],
        compiler_params=_params(("parallel", "parallel", "arbitrary")),
        name="hgrn2",
    )(p, p, p, p, lb, ng_, s0)


def _merge_kernel(h_ref, ga_ref, gb_ref, ya_ref, yb_ref, wa_ref, wb_ref, wo_ref, o_ref):
    a = jnp.dot(ya_ref[...], wa_ref[...], preferred_element_type=F32)
    b = jnp.dot(yb_ref[...], wb_ref[...], preferred_element_type=F32)
    merged = _sigmoid(ga_ref[...].astype(F32)) * a + _sigmoid(gb_ref[...].astype(F32)) * b
    o_ref[...] = h_ref[...] + jnp.dot(merged.astype(BF16), wo_ref[...], preferred_element_type=F32)


def _merge(h, p, ya, yb, wa, wb, wo, tt):
    ng, rows, _ = h.shape
    w = _resident((D_MODEL, D_MODEL))
    return pl.pallas_call(
        _merge_kernel,
        grid=(ng, rows // (tt * SUBLANES)),
        in_specs=[_rows(tt, D_MODEL), _rows(tt, D_MODEL, 5), _rows(tt, D_MODEL, 6),
                  _rows(tt, D_MODEL), _rows(tt, D_MODEL), w, w, w],
        out_specs=_rows(tt, D_MODEL),
        out_shape=jax.ShapeDtypeStruct((ng, rows, D_MODEL), F32),
        compiler_params=_params(("parallel", "parallel")),
        name="merge",
    )(h, p, p, ya, yb, wa, wb, wo)


def _ffn_kernel(h_ref, g_ref, wu_ref, cw_ref, cb_ref, wd_ref, gf_ref,
                c0_ref, o_ref, cfin_ref, carry_ref, act_ref, slab_ref, *, tt):
    ti = pl.program_id(1)

    @pl.when(ti == 0)
    def _():
        carry_ref[...] = c0_ref[...]

    rows = tt * SUBLANES
    tail = (CONV_WIDTH - 1) * SUBLANES
    h = h_ref[...]
    z = _rmsnorm(h, g_ref[...]).astype(BF16)

    def conv_up(n):
        cols = slice(n * FF_CHUNK, (n + 1) * FF_CHUNK)
        pre = jnp.dot(z, wu_ref[:, cols], preferred_element_type=F32)
        ext = jnp.concatenate([carry_ref[n], pre], axis=0)
        carry_ref[n] = ext[rows:rows + tail]
        out = cb_ref[:, cols]
        for j in range(CONV_WIDTH):
            out = out + ext[j * SUBLANES:j * SUBLANES + rows] * cw_ref[j:j + 1, cols]
        return out

    for n in range(N_FF_CHUNKS):
        gate = conv_up(n)
        val = conv_up(N_FF_CHUNKS + n)
        act_ref[:, n * FF_CHUNK:(n + 1) * FF_CHUNK] = (_silu(gate) * val).astype(BF16)

    down = jnp.dot(act_ref[...], wd_ref[...], preferred_element_type=F32)
    out = _rmsnorm(h + down, gf_ref[...])
    for j in range(N_LANE_SLABS):
        slab_ref[j] = out[:, j * LANES:(j + 1) * LANES]
    for b in range(SUBLANES):
        for j in range(N_LANE_SLABS):
            o_ref[b, :, j * LANES:(j + 1) * LANES] = slab_ref[j, pl.ds(b, tt, stride=SUBLANES), :]

    @pl.when(ti == pl.num_programs(1) - 1)
    def _():
        cfin_ref[...] = carry_ref[...]


def _ffn(h, g, wu, cw, cb, wd, gf, c0, tt):
    ng, rows, _ = h.shape
    l = rows // SUBLANES
    carry_shape = (2 * N_FF_CHUNKS, (CONV_WIDTH - 1) * SUBLANES, FF_CHUNK)
    return pl.pallas_call(
        functools.partial(_ffn_kernel, tt=tt),
        grid=(ng, l // tt),
        in_specs=[
            _rows(tt, D_MODEL),
            _resident((1, D_MODEL)),
            _resident(wu.shape), _resident(cw.shape), _resident(cb.shape), _resident(wd.shape),
            _resident((1, D_MODEL)),
            pl.BlockSpec((None,) + carry_shape, lambda g, t: (0, 0, 0, 0)),
        ],
        out_specs=[
            pl.BlockSpec((SUBLANES, tt, D_MODEL), lambda g, t: (g, t, 0)),
            pl.BlockSpec((None,) + carry_shape, lambda g, t: (g, 0, 0, 0)),
        ],
        out_shape=[
            jax.ShapeDtypeStruct((ng * SUBLANES, l, D_MODEL), F32),
            jax.ShapeDtypeStruct((ng,) + carry_shape, F32),
        ],
        scratch_shapes=[
            pltpu.VMEM(carry_shape, F32),
            pltpu.VMEM((tt * SUBLANES, D_FF), BF16),
            pltpu.VMEM((N_LANE_SLABS, tt * SUBLANES, LANES), F32),
        ],
        compiler_params=_params(("parallel", "arbitrary")),
        name="ffn",
    )(h, g, wu, cw, cb, wd, gf, c0)


def _s5_params(lam_re, lam_im, log_dt, b_re, b_im, c_re, c_im):
    hp = lax.Precision.HIGHEST
    t_len = SSM_CHUNK
    dt = jnp.exp(log_dt)[:, None]
    mag = jnp.exp(lam_re * dt)
    ab_re = mag * jnp.cos(lam_im * dt)
    ab_im = mag * jnp.sin(lam_im * dt)
    den = lam_re * lam_re + lam_im * lam_im
    nr = ab_re - 1.0
    coef_re = (nr * lam_re + ab_im * lam_im) / den
    coef_im = (ab_im * lam_re - nr * lam_im) / den
    bb_re = coef_re[..., None] * b_re - coef_im[..., None] * b_im
    bb_im = coef_re[..., None] * b_im + coef_im[..., None] * b_re
    tau = jnp.arange(t_len + 1, dtype=F32)[None, :, None]
    pw_mag = jnp.exp((lam_re * dt)[:, None, :] * tau)
    pw_re = pw_mag * jnp.cos((lam_im * dt)[:, None, :] * tau)
    pw_im = pw_mag * jnp.sin((lam_im * dt)[:, None, :] * tau)
    ab_b_re = pw_re[..., None] * bb_re[:, None] - pw_im[..., None] * bb_im[:, None]
    ab_b_im = pw_re[..., None] * bb_im[:, None] + pw_im[..., None] * bb_re[:, None]
    kern = (jnp.einsum("gop,gtpi->gtoi", c_re, ab_b_re[:, :t_len], precision=hp)
            - jnp.einsum("gop,gtpi->gtoi", c_im, ab_b_im[:, :t_len], precision=hp))
    out_re = c_re[:, None] * pw_re[:, :, None, :] - c_im[:, None] * pw_im[:, :, None, :]
    out_im = -(c_re[:, None] * pw_im[:, :, None, :] + c_im[:, None] * pw_re[:, :, None, :])

    half = t_len // 2
    slots = np.arange(t_len)
    step_of = np.stack([half * (slots // half) + (slots % half - gg) % half
                        for gg in range(GROUPS_PER_SLAB)])
    lags = np.arange(t_len + 1)
    inc_sel = (t_len - 1 - step_of[:, :, None]) == lags[None, None, :t_len]
    out_sel = (step_of[:, :, None] + 1) == lags[None, None, :]
    by_slab = lambda a: a.astype(BF16).reshape((N_LANE_SLABS, GROUPS_PER_SLAB) + a.shape[1:])
    select = lambda spec, sel, a: jnp.einsum(spec, jnp.asarray(sel, BF16), by_slab(a),
                                             preferred_element_type=F32).astype(BF16)
    dims = (GROUPS_PER_SLAB, SSM_FLAT, t_len, SSM_FLAT)
    x_i, k_i, r_i, c_i = (lax.broadcasted_iota(jnp.int32, dims, d) for d in range(4))
    step_at = lambda s: half * (s // half) + (s % half - x_i) % half
    lag_eye = ((step_at(c_i // SSM_GROUP) - step_at(r_i) == k_i // SSM_GROUP)
               & (k_i % SSM_GROUP == c_i % SSM_GROUP)).astype(BF16)
    kern_rows = by_slab(kern.transpose(0, 3, 1, 2).reshape(SSM_GROUPS, SSM_GROUP, SSM_FLAT))
    toep = jnp.einsum("jxik,xkrC->jxriC", kern_rows, lag_eye, preferred_element_type=F32)
    toep = toep.astype(BF16).reshape(SSM_GROUPS, SSM_FLAT, SSM_FLAT)
    inc_re = select("xsu,jxuph->jxshp", inc_sel, ab_b_re[:, :t_len])
    inc_im = select("xsu,jxuph->jxshp", inc_sel, ab_b_im[:, :t_len])
    inc = jnp.concatenate([inc_re, inc_im, inc_im, inc_re], axis=-1)
    w1 = jnp.concatenate([toep, inc.reshape(SSM_GROUPS, SSM_FLAT, 2 * SSM_STATE_LANES)], axis=-1)
    ms = jnp.concatenate([select("xsu,jxuop->jxpso", out_sel, out_re),
                          select("xsu,jxuop->jxpso", out_sel, out_im)], axis=2)
    ms = ms.reshape(SSM_GROUPS, SSM_STATE_LANES, SSM_FLAT)
    a16_re, a16_im = pw_re[:, t_len], pw_im[:, t_len]
    c1 = jnp.concatenate([a16_re, a16_re], axis=-1)
    c2 = jnp.concatenate([-a16_im, a16_im], axis=-1)
    bcast = lambda a: jnp.broadcast_to(a[:, None, :], (SSM_GROUPS, SUBLANES, SSM_STATE_LANES))
    return w1, ms, bcast(c1), bcast(c2)


def _block(x, states, w):
    l = x.shape[1]
    s5_state, hgrn_state, conv_carry = states
    t_in, t_s5, t_hgrn, t_merge, t_ffn = (min(t, l) for t in TILES)
    p, h = _inproj(x, w["mix_g"], w["w_in"], t_in)
    ya, s5_state = _s5(p, w["s5_w1"], w["s5_ms"], w["s5_c1"], w["s5_c2"], w["d_skip"], w["w_glu"],
                       s5_state, t_s5)
    yb, hgrn_state = _hgrn(p, w["lb"], w["hgrn_g"], hgrn_state, t_hgrn, min(HGRN_CHUNK, l))
    h1 = _merge(h, p, ya, yb, w["w_a"], w["w_b"], w["w_out"], t_merge)
    out, conv_carry = _ffn(h1, w["ffn_g"], w["w_up"], w["conv_w"], w["conv_b"], w["w_down"],
                           w["final_g"], conv_carry, t_ffn)
    return out, (s5_state, hgrn_state, conv_carry)


def kernel(x, meta_tokens, mix_norm_g, w_in, ssm_lambda_re, ssm_lambda_im, ssm_log_dt, ssm_b_re, ssm_b_im, ssm_c_re, ssm_c_im, ssm_d, ssm_w_glu, w_ssm_proj, hgrn_lb_logits, hgrn_norm_g, w_hgrn_proj, w_out, ffn_norm_g, w_up, conv_w, conv_b, w_down, final_norm_g):
    assert x.shape[0] % SUBLANES == 0 and x.shape[1] % SSM_CHUNK == 0 and x.shape[2] == D_MODEL
    s5_w1, s5_ms, s5_c1, s5_c2 = _s5_params(ssm_lambda_re[0], ssm_lambda_im[0], ssm_log_dt[0],
                                            ssm_b_re[0], ssm_b_im[0], ssm_c_re[0], ssm_c_im[0])
    lower_bounds = jnp.cumsum(jax.nn.softmax(hgrn_lb_logits.astype(F32), axis=0), axis=0)
    w = {
        "mix_g": mix_norm_g[0].reshape(1, D_MODEL),
        "w_in": w_in[0].astype(BF16),
        "s5_w1": s5_w1, "s5_ms": s5_ms, "s5_c1": s5_c1, "s5_c2": s5_c2,
        "d_skip": ssm_d[0].reshape(1, D_MODEL),
        "w_glu": ssm_w_glu[0].astype(BF16),
        "lb": lower_bounds[0].reshape(1, D_MODEL),
        "hgrn_g": jnp.tile(hgrn_norm_g[0], HGRN_HEADS).reshape(1, D_MODEL),
        "w_a": w_ssm_proj[0].astype(BF16),
        "w_b": w_hgrn_proj[0].astype(BF16),
        "w_out": w_out[0].astype(BF16),
        "ffn_g": ffn_norm_g[0].reshape(1, D_MODEL),
        "w_up": w_up[0].astype(BF16),
        "conv_w": conv_w[0],
        "conv_b": conv_b[0].reshape(1, 2 * D_FF),
        "w_down": w_down[0].astype(BF16),
        "final_g": final_norm_g.reshape(1, D_MODEL),
    }
    zero_states = (
        jnp.zeros((1, 2, SSM_GROUPS, SUBLANES, SSM_STATE_LANES), F32),
        jnp.zeros((1, HGRN_HEADS, SUBLANES, HGRN_HEAD_DIM, HGRN_HEAD_DIM), F32),
        jnp.zeros((1, 2 * N_FF_CHUNKS, (CONV_WIDTH - 1) * SUBLANES, FF_CHUNK), F32),
    )
    meta = jnp.broadcast_to(meta_tokens.astype(x.dtype)[None], (SUBLANES, N_META, D_MODEL))
    _, states = _block(meta, zero_states, w)
    out, _ = _block(x, states, w)
    return out
```

```python
import functools
import math

import numpy as np
import jax
import jax.numpy as jnp
from jax import lax
from jax.experimental import pallas as pl
from jax.experimental.pallas import tpu as pltpu

F32 = jnp.float32
BF16 = jnp.bfloat16

D_MODEL = 1024
N_META = 16
SSM_GROUPS = 64
SSM_GROUP = 16
SSM_STATE = 64
SSM_CHUNK = 16
SSM_FLAT = SSM_CHUNK * SSM_GROUP
SSM_STATE_LANES = 2 * SSM_STATE
HGRN_HEADS = 8
HGRN_HEAD_DIM = 128
HGRN_CHUNK = 32
HGRN_HEADS_PER_STEP = 2
D_FF = 2816
FF_CHUNK = 256
N_FF_CHUNKS = D_FF // FF_CHUNK
CONV_WIDTH = 3
IN_BLOCKS = 7
EPS = 1e-6
GELU_C0 = math.sqrt(2.0 / math.pi)
GELU_C1 = 0.044715 * GELU_C0
LANES = 128
SUBLANES = 8
N_LANE_SLABS = D_MODEL // LANES
GROUPS_PER_SLAB = LANES // SSM_GROUP
VMEM_LIMIT_BYTES = 56 * 1024 * 1024
TILES = (64, 128, 256, 128, 128)


def _rmsnorm(x, g):
    return x * lax.rsqrt(jnp.mean(x * x, axis=-1, keepdims=True) + EPS) * g


def _sigmoid(x):
    return 0.5 * jnp.tanh(0.5 * x) + 0.5


def _silu(x):
    h = 0.5 * x
    return h + h * jnp.tanh(h)


def _gelu_tanh(x):
    h = 0.5 * x
    return h + h * jnp.tanh(x * (GELU_C0 + GELU_C1 * (x * x)))


def _resident(shape):
    n = len(shape)
    return pl.BlockSpec(shape, lambda *_: (0,) * n, pipeline_mode=pl.Buffered(1))


def _params(semantics):
    return pltpu.CompilerParams(dimension_semantics=semantics, vmem_limit_bytes=VMEM_LIMIT_BYTES)


def _rows(tt, cols, k=0):
    return pl.BlockSpec((None, tt * SUBLANES, cols), lambda g, t, k=k: (g, t, k))


def _inproj_kernel(x_ref, g_ref, w_ref, p_ref, h_ref, slab_ref, *, tt):
    for b in range(SUBLANES):
        for j in range(N_LANE_SLABS):
            slab_ref[j, pl.ds(b, tt, stride=SUBLANES), :] = x_ref[b, :, j * LANES:(j + 1) * LANES]
    h = jnp.concatenate([slab_ref[j] for j in range(N_LANE_SLABS)], axis=1)
    h_ref[...] = h
    z = _rmsnorm(h, g_ref[...]).astype(BF16)
    for k in range(IN_BLOCKS):
        cols = slice(k * D_MODEL, (k + 1) * D_MODEL)
        p_ref[:, cols] = jnp.dot(z, w_ref[:, cols], preferred_element_type=F32).astype(BF16)


def _inproj(x, g, w, tt):
    b, l, _ = x.shape
    ng = b // SUBLANES
    return pl.pallas_call(
        functools.partial(_inproj_kernel, tt=tt),
        grid=(ng, l // tt),
        in_specs=[
            pl.BlockSpec((SUBLANES, tt, D_MODEL), lambda i, t: (i, t, 0)),
            _resident((1, D_MODEL)),
            _resident((D_MODEL, IN_BLOCKS * D_MODEL)),
        ],
        out_specs=[_rows(tt, IN_BLOCKS * D_MODEL), _rows(tt, D_MODEL)],
        out_shape=[
            jax.ShapeDtypeStruct((ng, l * SUBLANES, IN_BLOCKS * D_MODEL), BF16),
            jax.ShapeDtypeStruct((ng, l * SUBLANES, D_MODEL), F32),
        ],
        scratch_shapes=[pltpu.VMEM((N_LANE_SLABS, tt * SUBLANES, LANES), F32)],
        compiler_params=_params(("parallel", "parallel")),
        name="inproj",
    )(x, g, w)


def _s5_kernel(u_ref, w1_ref, ms_ref, c1_ref, c2_ref, d_ref, wglu_ref, s0_ref,
               ya_ref, sfin_ref, st_ref, z_ref, o_ref, y_ref, *, tt):
    ti = pl.program_id(1)

    @pl.when(ti == 0)
    def _():
        st_ref[...] = s0_ref[...]

    nc = tt // SSM_CHUNK
    m = nc * SUBLANES
    chunk_rows = SSM_CHUNK * SUBLANES
    half = SSM_CHUNK // 2
    lane_slot = lax.broadcasted_iota(jnp.int32, (m, LANES), 1) // SSM_GROUP
    in_slot = [lane_slot == s for s in range(GROUPS_PER_SLAB)]

    u = u_ref[...].astype(F32)

    def step_rows(x, t, j):
        return jnp.concatenate(
            [x[c * chunk_rows + t * SUBLANES:c * chunk_rows + (t + 1) * SUBLANES, j * LANES:(j + 1) * LANES]
             for c in range(nc)], axis=0)

    def slab_outputs(j):
        groups = range(j * GROUPS_PER_SLAB, (j + 1) * GROUPS_PER_SLAB)
        gs = slice(j * GROUPS_PER_SLAB, (j + 1) * GROUPS_PER_SLAB)
        rs = [jnp.dot(z_ref[g], w1_ref[g], preferred_element_type=F32) for g in groups]
        s, sw = st_ref[0, gs], st_ref[1, gs]
        c1, c2 = c1_ref[gs], c2_ref[gs]
        carried = []
        for c in range(nc):
            carried.append(s)
            rows_c = slice(c * SUBLANES, (c + 1) * SUBLANES)
            inc = jnp.stack([r[rows_c, SSM_FLAT:SSM_FLAT + SSM_STATE_LANES] for r in rs])
            inc_sw = jnp.stack([r[rows_c, SSM_FLAT + SSM_STATE_LANES:] for r in rs])
            s, sw = c1 * s + c2 * sw + inc, c1 * sw - c2 * s + inc_sw
        st_ref[0, gs] = s
        st_ref[1, gs] = sw
        for i, g in enumerate(groups):
            before = jnp.concatenate([cs[i] for cs in carried], axis=0).astype(BF16)
            o_ref[g] = rs[i][:, :SSM_FLAT] + jnp.dot(before, ms_ref[g], preferred_element_type=F32)

    for j in range(N_LANE_SLABS):
        rot = []
        for t in range(SSM_CHUNK):
            k = t % half
            x = step_rows(u, t, j)
            rot.append(x if k == 0 else pltpu.roll(x, SSM_GROUP * k, axis=1))
        for gg in range(GROUPS_PER_SLAB):
            for hf in range(2):
                acc = rot[half * hf]
                for k in range(1, half):
                    acc = jnp.where(in_slot[(gg + k) % half], rot[half * hf + k], acc)
                z_ref[j * GROUPS_PER_SLAB + gg, :, hf * LANES:(hf + 1) * LANES] = acc.astype(BF16)

    for j in range(N_LANE_SLABS):
        slab_outputs(j)

    for j in range(N_LANE_SLABS):
        for hf in range(2):
            ys = [o_ref[j * GROUPS_PER_SLAB + gg, :, hf * LANES:(hf + 1) * LANES]
                  for gg in range(GROUPS_PER_SLAB)]
            for k in range(half):
                acc = ys[0]
                for gg in range(1, GROUPS_PER_SLAB):
                    acc = jnp.where(in_slot[(gg + k) % half], ys[gg], acc)
                nat = acc if k == 0 else pltpu.roll(acc, LANES - SSM_GROUP * k, axis=1)
                t = half * hf + k
                for c in range(nc):
                    y_ref[c * chunk_rows + t * SUBLANES:c * chunk_rows + (t + 1) * SUBLANES,
                          j * LANES:(j + 1) * LANES] = nat[c * SUBLANES:(c + 1) * SUBLANES]

    y = _gelu_tanh(y_ref[...] + d_ref[...] * u)
    gate = jnp.dot(y.astype(BF16), wglu_ref[...], preferred_element_type=F32)
    ya_ref[...] = (y * _sigmoid(gate)).astype(BF16)

    @pl.when(ti == pl.num_programs(1) - 1)
    def _():
        sfin_ref[...] = st_ref[...]


def _s5(p, w1, ms, c1, c2, d_skip, w_glu, s0, tt):
    ng, rows, _ = p.shape
    m = tt // SSM_CHUNK * SUBLANES
    state_shape = (2, SSM_GROUPS, SUBLANES, SSM_STATE_LANES)
    return pl.pallas_call(
        functools.partial(_s5_kernel, tt=tt),
        grid=(ng, rows // (tt * SUBLANES)),
        in_specs=[
            _rows(tt, D_MODEL, 0),
            _resident(w1.shape),
            _resident(ms.shape),
            _resident(c1.shape),
            _resident(c2.shape),
            _resident((1, D_MODEL)),
            _resident((D_MODEL, D_MODEL)),
            pl.BlockSpec((None,) + state_shape, lambda g, t: (0, 0, 0, 0, 0)),
        ],
        out_specs=[
            _rows(tt, D_MODEL),
            pl.BlockSpec((None,) + state_shape, lambda g, t: (g, 0, 0, 0, 0)),
        ],
        out_shape=[
            jax.ShapeDtypeStruct((ng, rows, D_MODEL), BF16),
            jax.ShapeDtypeStruct((ng,) + state_shape, F32),
        ],
        scratch_shapes=[
            pltpu.VMEM(state_shape, F32),
            pltpu.VMEM((SSM_GROUPS, m, SSM_FLAT), BF16),
            pltpu.VMEM((SSM_GROUPS, m, SSM_FLAT), F32),
            pltpu.VMEM((tt * SUBLANES, D_MODEL), F32),
        ],
        compiler_params=_params(("parallel", "arbitrary")),
        name="s5",
    )(p, w1, ms, c1, c2, d_skip, w_glu, s0)


def _hgrn_kernel(q_ref, f_ref, i_ref, og_ref, lb_ref, ng_ref, s0_ref,
                 yb_ref, sfin_ref, st_ref, qin_ref, kout_ref, v_ref, oint_ref, *, tt, chunk, heads):
    ti = pl.program_id(2)

    @pl.when(ti == 0)
    def _():
        st_ref[...] = s0_ref[...]

    rows = chunk * SUBLANES
    r_idx = lax.broadcasted_iota(jnp.int32, (rows, rows), 0)
    c_idx = lax.broadcasted_iota(jnp.int32, (rows, rows), 1)
    allowed = (((r_idx - c_idx) & (SUBLANES - 1)) == 0) & (c_idx <= r_idx)
    mid = chunk // 2 - 1
    head = (chunk, SUBLANES, HGRN_HEAD_DIM)

    def decays(hh, ci):
        r = slice(ci * rows, (ci + 1) * rows)
        slot = ci % 2
        lanes = slice(hh * HGRN_HEAD_DIM, (hh + 1) * HGRN_HEAD_DIM)
        half_gap = 0.5 * (1.0 - lb_ref[:, lanes])
        swing = half_gap * jnp.tanh(0.5 * f_ref[r, lanes].astype(F32))
        f = (1.0 - half_gap) + swing
        k = (half_gap - swing).reshape(head)
        cum = jnp.log(f).reshape(head)
        shift = 1
        while shift < chunk:
            cum = cum + jnp.concatenate([jnp.zeros((shift,) + head[1:], F32), cum[:-shift]], axis=0)
            shift *= 2
        ref_row = cum[mid:mid + 1]
        last = cum[chunk - 1:chunk]
        q = q_ref[r, lanes].astype(F32).reshape(head)
        q_mid = q * jnp.exp(cum - ref_row)
        k_mid = k * jnp.exp(ref_row - cum)
        qin_ref[slot, hh] = (q_mid * jnp.exp(ref_row)).reshape(rows, HGRN_HEAD_DIM)
        kout_ref[slot, hh] = (k_mid * jnp.exp(last - ref_row)).reshape(rows, HGRN_HEAD_DIM)
        v = i_ref[r, lanes]
        v_ref[slot, hh] = v.astype(F32)
        return (q_mid.reshape(rows, HGRN_HEAD_DIM).astype(BF16),
                k_mid.reshape(rows, HGRN_HEAD_DIM).astype(BF16), v, jnp.exp(last[0]))

    rb =[pl.ds(b, chunk, stride=SUBLANES) for b in range(SUBLANES)]

    def matmuls(ci, operands):
        r = slice(ci * rows, (ci + 1) * rows)
        slot = ci % 2
        q_mid, k_mid, v, decay = zip(*operands)
        scores = [lax.dot_general(q_mid[hh], k_mid[hh], (((1,), (1,)), ((), ())),
                                  preferred_element_type=F32) for hh in range(heads)]
        scores = [jnp.where(allowed, s, 0.0).astype(BF16) for s in scores]
        o = [jnp.dot(scores[hh], v[hh], preferred_element_type=F32) for hh in range(heads)]
        no_keys = jnp.zeros((chunk, HGRN_HEAD_DIM), BF16)
        for b in range(SUBLANES):
            state_t = st_ref[b]
            state_bf = state_t.astype(BF16)
            values, keys = [], []
            for hh in range(heads):
                lanes = slice(hh * HGRN_HEAD_DIM, (hh + 1) * HGRN_HEAD_DIM)
                oint_ref[slot, hh, rb[b], :] = lax.dot_general(
                    qin_ref[slot, hh, rb[b], :].astype(BF16), state_bf[:, lanes],
                    (((1,), (1,)), ((), ())), preferred_element_type=F32)
                values.append(v_ref[slot, hh, rb[b], :].astype(BF16))
                k_out = kout_ref[slot, hh, rb[b], :].astype(BF16)
                keys.append(jnp.concatenate([k_out if other == hh else no_keys for other in range(heads)],
                                            axis=1))
            update = lax.dot_general(jnp.concatenate(values, axis=0), jnp.concatenate(keys, axis=0),
                                     (((0,), (0,)), ((), ())), preferred_element_type=F32)
            decay_b = jnp.concatenate([decay[hh][b:b + 1, :] for hh in range(heads)], axis=1)
            st_ref[b] = state_t * decay_b + update
        for hh in range(heads):
            lanes = slice(hh * HGRN_HEAD_DIM, (hh + 1) * HGRN_HEAD_DIM)
            out = _rmsnorm(o[hh] + oint_ref[slot, hh], ng_ref[:, lanes])
            yb_ref[r, lanes] = (out * _silu(og_ref[r, lanes].astype(F32))).astype(BF16)

    n_chunks = tt // chunk
    operands = [decays(hh, 0) for hh in range(heads)]
    for ci in range(n_chunks):
        following = [decays(hh, ci + 1) for hh in range(heads)] if ci + 1 < n_chunks else None
        matmuls(ci, operands)
        operands = following

    @pl.when(ti == pl.num_programs(2) - 1)
    def _():
        sfin_ref[...] = st_ref[...]


def _hgrn(p, lb, ng_, s0, tt, chunk):
    ng, rows, _ = p.shape
    heads = HGRN_HEADS_PER_STEP
    width = heads * HGRN_HEAD_DIM
    steps = HGRN_HEADS // heads
    col = lambda k: pl.BlockSpec((None, tt * SUBLANES, width), lambda g, h, t, k=k: (g, t, k * steps + h))
    per_head = pl.BlockSpec((1, width), lambda g, h, t: (0, h))
    state_shape = (None, SUBLANES, HGRN_HEAD_DIM, width)
    chunk_buf = pltpu.VMEM((2, heads, chunk * SUBLANES, HGRN_HEAD_DIM), F32)
    return pl.pallas_call(
        functools.partial(_hgrn_kernel, tt=tt, chunk=chunk, heads=heads),
        grid=(ng, steps, rows // (tt * SUBLANES)),
        in_specs=[col(1), col(2), col(3), col(4), per_head, per_head,
                  pl.BlockSpec((None,) + state_shape, lambda g, h, t: (0, h, 0, 0, 0))],
        out_specs=[
            pl.BlockSpec((None, tt * SUBLANES, width), lambda g, h, t: (g, t, h)),
            pl.BlockSpec((None,) + state_shape, lambda g, h, t: (g, h, 0, 0, 0)),
        ],
        out_shape=[
            jax.ShapeDtypeStruct((ng, rows, D_MODEL), BF16),
            jax.ShapeDtypeStruct((ng, steps) + state_shape[1:], F32),
        ],
        scratch_shapes=[pltpu.VMEM(state_shape[1:], F32), chunk_buf, chunk_buf, chunk_buf, chunk_buf],
        compiler_params=_params(("parallel", "parallel", "arbitrary")),
        name="hgrn2",
    )(p, p, p, p, lb, ng_, s0)


def _merge_kernel(h_ref, ga_ref, gb_ref, ya_ref, yb_ref, wa_ref, wb_ref, wo_ref, o_ref):
    a = jnp.dot(ya_ref[...], wa_ref[...], preferred_element_type=F32)
    b = jnp.dot(yb_ref[...], wb_ref[...], preferred_element_type=F32)
    merged = _sigmoid(ga_ref[...].astype(F32)) * a + _sigmoid(gb_ref[...].astype(F32)) * b
    o_ref[...] = h_ref[...] + jnp.dot(merged.astype(BF16), wo_ref[...], preferred_element_type=F32)


def _merge(h, p, ya, yb, wa, wb, wo, tt):
    ng, rows, _ = h.shape
    w = _resident((D_MODEL, D_MODEL))
    return pl.pallas_call(
        _merge_kernel,
        grid=(ng, rows // (tt * SUBLANES)),
        in_specs=[_rows(tt, D_MODEL), _rows(tt, D_MODEL, 5), _rows(tt, D_MODEL, 6),
                  _rows(tt, D_MODEL), _rows(tt, D_MODEL), w, w, w],
        out_specs=_rows(tt, D_MODEL),
        out_shape=jax.ShapeDtypeStruct((ng, rows, D_MODEL), F32),
        compiler_params=_params(("parallel", "parallel")),
        name="merge",
    )(h, p, p, ya, yb, wa, wb, wo)


def _ffn_kernel(h_ref, g_ref, wu_ref, cw_ref, cb_ref, wd_ref, gf_ref,
                c0_ref, o_ref, cfin_ref, carry_ref, act_ref, slab_ref, *, tt):
    ti = pl.program_id(1)

    @pl.when(ti == 0)
    def _():
        carry_ref[...] = c0_ref[...]

    rows = tt * SUBLANES
    tail = (CONV_WIDTH - 1) * SUBLANES
    h = h_ref[...]
    z = _rmsnorm(h, g_ref[...]).astype(BF16)

    def conv_up(n):
        cols = slice(n * FF_CHUNK, (n + 1) * FF_CHUNK)
        pre = jnp.dot(z, wu_ref[:, cols], preferred_element_type=F32)
        ext = jnp.concatenate([carry_ref[n], pre], axis=0)
        carry_ref[n] = ext[rows:rows + tail]
        out = cb_ref[:, cols]
        for j in range(CONV_WIDTH):
            out = out + ext[j * SUBLANES:j * SUBLANES + rows] * cw_ref[j:j + 1, cols]
        return out

    for n in range(N_FF_CHUNKS):
        gate = conv_up(n)
        val = conv_up(N_FF_CHUNKS + n)
        act_ref[:, n * FF_CHUNK:(n + 1) * FF_CHUNK] = (_silu(gate) * val).astype(BF16)

    down = jnp.dot(act_ref[...], wd_ref[...], preferred_element_type=F32)
    out = _rmsnorm(h + down, gf_ref[...])
    for j in range(N_LANE_SLABS):
        slab_ref[j] = out[:, j * LANES:(j + 1) * LANES]
    for b in range(SUBLANES):
        for j in range(N_LANE_SLABS):
            o_ref[b, :, j * LANES:(j + 1) * LANES] = slab_ref[j, pl.ds(b, tt, stride=SUBLANES), :]

    @pl.when(ti == pl.num_programs(1) - 1)
    def _():
        cfin_ref[...] = carry_ref[...]


def _ffn(h, g, wu, cw, cb, wd, gf, c0, tt):
    ng, rows, _ = h.shape
    l = rows // SUBLANES
    carry_shape = (2 * N_FF_CHUNKS, (CONV_WIDTH - 1) * SUBLANES, FF_CHUNK)
    return pl.pallas_call(
        functools.partial(_ffn_kernel, tt=tt),
        grid=(ng, l // tt),
        in_specs=[
            _rows(tt, D_MODEL),
            _resident((1, D_MODEL)),
            _resident(wu.shape), _resident(cw.shape), _resident(cb.shape), _resident(wd.shape),
            _resident((1, D_MODEL)),
            pl.BlockSpec((None,) + carry_shape, lambda g, t: (0, 0, 0, 0)),
        ],
        out_specs=[
            pl.BlockSpec((SUBLANES, tt, D_MODEL), lambda g, t: (g, t, 0)),
            pl.BlockSpec((None,) + carry_shape, lambda g, t: (g, 0, 0, 0)),
        ],
        out_shape=[
            jax.ShapeDtypeStruct((ng * SUBLANES, l, D_MODEL), F32),
            jax.ShapeDtypeStruct((ng,) + carry_shape, F32),
        ],
        scratch_shapes=[
            pltpu.VMEM(carry_shape, F32),
            pltpu.VMEM((tt * SUBLANES, D_FF), BF16),
            pltpu.VMEM((N_LANE_SLABS, tt * SUBLANES, LANES), F32),
        ],
        compiler_params=_params(("parallel", "arbitrary")),
        name="ffn",
    )(h, g, wu, cw, cb, wd, gf, c0)


def _s5_params(lam_re, lam_im, log_dt, b_re, b_im, c_re, c_im):
    hp = lax.Precision.HIGHEST
    t_len = SSM_CHUNK
    dt = jnp.exp(log_dt)[:, None]
    mag = jnp.exp(lam_re * dt)
    ab_re = mag * jnp.cos(lam_im * dt)
    ab_im = mag * jnp.sin(lam_im * dt)
    den = lam_re * lam_re + lam_im * lam_im
    nr = ab_re - 1.0
    coef_re = (nr * lam_re + ab_im * lam_im) / den
    coef_im = (ab_im * lam_re - nr * lam_im) / den
    bb_re = coef_re[..., None] * b_re - coef_im[..., None] * b_im
    bb_im = coef_re[..., None] * b_im + coef_im[..., None] * b_re
    tau = jnp.arange(t_len + 1, dtype=F32)[None, :, None]
    pw_mag = jnp.exp((lam_re * dt)[:, None, :] * tau)
    pw_re = pw_mag * jnp.cos((lam_im * dt)[:, None, :] * tau)
    pw_im = pw_mag * jnp.sin((lam_im * dt)[:, None, :] * tau)
    ab_b_re = pw_re[..., None] * bb_re[:, None] - pw_im[..., None] * bb_im[:, None]
    ab_b_im = pw_re[..., None] * bb_im[:, None] + pw_im[..., None] * bb_re[:, None]
    kern = (jnp.einsum("gop,gtpi->gtoi", c_re, ab_b_re[:, :t_len], precision=hp)
            - jnp.einsum("gop,gtpi->gtoi", c_im, ab_b_im[:, :t_len], precision=hp))
    out_re = c_re[:, None] * pw_re[:, :, None, :] - c_im[:, None] * pw_im[:, :, None, :]
    out_im = -(c_re[:, None] * pw_im[:, :, None, :] + c_im[:, None] * pw_re[:, :, None, :])

    half = t_len // 2
    slots = np.arange(t_len)
    step_of = np.stack([half * (slots // half) + (slots % half - gg) % half
                        for gg in range(GROUPS_PER_SLAB)])
    lags = np.arange(t_len + 1)
    inc_sel = (t_len - 1 - step_of[:, :, None]) == lags[None, None, :t_len]
    out_sel = (step_of[:, :, None] + 1) == lags[None, None, :]
    by_slab = lambda a: a.astype(BF16).reshape((N_LANE_SLABS, GROUPS_PER_SLAB) + a.shape[1:])
    select = lambda spec, sel, a: jnp.einsum(spec, jnp.asarray(sel, BF16), by_slab(a),
                                             preferred_element_type=F32).astype(BF16)
    dims = (GROUPS_PER_SLAB, SSM_FLAT, t_len, SSM_FLAT)
    x_i, k_i, r_i, c_i = (lax.broadcasted_iota(jnp.int32, dims, d) for d in range(4))
    step_at = lambda s: half * (s // half) + (s % half - x_i) % half
    lag_eye = ((step_at(c_i // SSM_GROUP) - step_at(r_i) == k_i // SSM_GROUP)
               & (k_i % SSM_GROUP == c_i % SSM_GROUP)).astype(BF16)
    kern_rows = by_slab(kern.transpose(0, 3, 1, 2).reshape(SSM_GROUPS, SSM_GROUP, SSM_FLAT))
    toep = jnp.einsum("jxik,xkrC->jxriC", kern_rows, lag_eye, preferred_element_type=F32)
    toep = toep.astype(BF16).reshape(SSM_GROUPS, SSM_FLAT, SSM_FLAT)
    inc_re = select("xsu,jxuph->jxshp", inc_sel, ab_b_re[:, :t_len])
    inc_im = select("xsu,jxuph->jxshp", inc_sel, ab_b_im[:, :t_len])
    inc = jnp.concatenate([inc_re, inc_im, inc_im, inc_re], axis=-1)
    w1 = jnp.concatenate([toep, inc.reshape(SSM_GROUPS, SSM_FLAT, 2 * SSM_STATE_LANES)], axis=-1)
    ms = jnp.concatenate([select("xsu,jxuop->jxpso", out_sel, out_re),
                          select("xsu,jxuop->jxpso", out_sel, out_im)], axis=2)
    ms = ms.reshape(SSM_GROUPS, SSM_STATE_LANES, SSM_FLAT)
    a16_re, a16_im = pw_re[:, t_len], pw_im[:, t_len]
    c1 = jnp.concatenate([a16_re, a16_re], axis=-1)
    c2 = jnp.concatenate([-a16_im, a16_im], axis=-1)
    bcast = lambda a: jnp.broadcast_to(a[:, None, :], (SSM_GROUPS, SUBLANES, SSM_STATE_LANES))
    return w1, ms, bcast(c1), bcast(c2)


def _block(x, states, w):
    l = x.shape[1]
    s5_state, hgrn_state, conv_carry = states
    t_in, t_s5, t_hgrn, t_merge, t_ffn = (min(t, l) for t in TILES)
    p, h = _inproj(x, w["mix_g"], w["w_in"], t_in)
    ya, s5_state = _s5(p, w["s5_w1"], w["s5_ms"], w["s5_c1"], w["s5_c2"], w["d_skip"], w["w_glu"],
                       s5_state, t_s5)
    yb, hgrn_state = _hgrn(p, w["lb"], w["hgrn_g"], hgrn_state, t_hgrn, min(HGRN_CHUNK, l))
    h1 = _merge(h, p, ya, yb, w["w_a"], w["w_b"], w["w_out"], t_merge)
    out, conv_carry = _ffn(h1, w["ffn_g"], w["w_up"], w["conv_w"], w["conv_b"], w["w_down"],
                           w["final_g"], conv_carry, t_ffn)
    return out, (s5_state, hgrn_state, conv_carry)


def kernel(x, meta_tokens, mix_norm_g, w_in, ssm_lambda_re, ssm_lambda_im, ssm_log_dt, ssm_b_re, ssm_b_im, ssm_c_re, ssm_c_im, ssm_d, ssm_w_glu, w_ssm_proj, hgrn_lb_logits, hgrn_norm_g, w_hgrn_proj, w_out, ffn_norm_g, w_up, conv_w, conv_b, w_down, final_norm_g):
    assert x.shape[0] % SUBLANES == 0 and x.shape[1] % SSM_CHUNK == 0 and x.shape[2] == D_MODEL
    s5_w1, s5_ms, s5_c1, s5_c2 = _s5_params(ssm_lambda_re[0], ssm_lambda_im[0], ssm_log_dt[0],
                                            ssm_b_re[0], ssm_b_im[0], ssm_c_re[0], ssm_c_im[0])
    lower_bounds = jnp.cumsum(jax.nn.softmax(hgrn_lb_logits.astype(F32), axis=0), axis=0)
    w = {
        "mix_g": mix_norm_g[0].reshape(1, D_MODEL),
        "w_in": w_in[0].astype(BF16),
        "s5_w1": s5_w1, "s5_ms": s5_ms, "s5_c1": s5_c1, "s5_c2": s5_c2,
        "d_skip": ssm_d[0].reshape(1, D_MODEL),
        "w_glu": ssm_w_glu[0].astype(BF16),
        "lb": lower_bounds[0].reshape(1, D_MODEL),
        "hgrn_g": jnp.tile(hgrn_norm_g[0], HGRN_HEADS).reshape(1, D_MODEL),
        "w_a": w_ssm_proj[0].astype(BF16),
        "w_b": w_hgrn_proj[0].astype(BF16),
        "w_out": w_out[0].astype(BF16),
        "ffn_g": ffn_norm_g[0].reshape(1, D_MODEL),
        "w_up": w_up[0].astype(BF16),
        "conv_w": conv_w[0],
        "conv_b": conv_b[0].reshape(1, 2 * D_FF),
        "w_down": w_down[0].astype(BF16),
        "final_g": final_norm_g.reshape(1, D_MODEL),
    }
    zero_states = (
        jnp.zeros((1, 2, SSM_GROUPS, SUBLANES, SSM_STATE_LANES), F32),
        jnp.zeros((1, HGRN_HEADS // HGRN_HEADS_PER_STEP, SUBLANES, HGRN_HEAD_DIM,
                   HGRN_HEADS_PER_STEP * HGRN_HEAD_DIM), F32),
        jnp.zeros((1, 2 * N_FF_CHUNKS, (CONV_WIDTH - 1) * SUBLANES, FF_CHUNK), F32),
    )
    meta = jnp.broadcast_to(meta_tokens.astype(x.dtype)[None], (SUBLANES, N_META, D_MODEL))
    _, states = _block(meta, zero_states, w)
    out, _ = _block(x, states, w)
    return out
```

```python
import functools
import math

import numpy as np
import jax
import jax.numpy as jnp
from jax import lax
from jax.experimental import pallas as pl
from jax.experimental.pallas import tpu as pltpu

F32 = jnp.float32
BF16 = jnp.bfloat16

D_MODEL = 1024
N_META = 16
SSM_GROUPS = 64
SSM_GROUP = 16
SSM_STATE = 64
SSM_CHUNK = 16
SSM_FLAT = SSM_CHUNK * SSM_GROUP
SSM_STATE_LANES = 2 * SSM_STATE
HGRN_HEADS = 8
HGRN_HEAD_DIM = 128
HGRN_CHUNK = 32
HGRN_HEADS_PER_STEP = 2
D_FF = 2816
FF_CHUNK = 256
N_FF_CHUNKS = D_FF // FF_CHUNK
CONV_WIDTH = 3
IN_BLOCKS = 7
EPS = 1e-6
GELU_C0 = math.sqrt(2.0 / math.pi)
GELU_C1 = 0.044715 * GELU_C0
LANES = 128
SUBLANES = 8
N_LANE_SLABS = D_MODEL // LANES
GROUPS_PER_SLAB = LANES // SSM_GROUP
VMEM_LIMIT_BYTES = 56 * 1024 * 1024
TILES = (64, 128, 512, 128, 128)


def _rmsnorm(x, g):
    return x * lax.rsqrt(jnp.mean(x * x, axis=-1, keepdims=True) + EPS) * g


def _sigmoid(x):
    return 0.5 * jnp.tanh(0.5 * x) + 0.5


def _silu(x):
    h = 0.5 * x
    return h + h * jnp.tanh(h)


def _gelu_tanh(x):
    h = 0.5 * x
    return h + h * jnp.tanh(x * (GELU_C0 + GELU_C1 * (x * x)))


def _resident(shape):
    n = len(shape)
    return pl.BlockSpec(shape, lambda *_: (0,) * n, pipeline_mode=pl.Buffered(1))


def _params(semantics):
    return pltpu.CompilerParams(dimension_semantics=semantics, vmem_limit_bytes=VMEM_LIMIT_BYTES)


def _rows(tt, cols, k=0):
    return pl.BlockSpec((None, tt * SUBLANES, cols), lambda g, t, k=k: (g, t, k))


def _inproj_kernel(x_ref, g_ref, w_ref, p_ref, h_ref, slab_ref, *, tt):
    for b in range(SUBLANES):
        for j in range(N_LANE_SLABS):
            slab_ref[j, pl.ds(b, tt, stride=SUBLANES), :] = x_ref[b, :, j * LANES:(j + 1) * LANES]
    h = jnp.concatenate([slab_ref[j] for j in range(N_LANE_SLABS)], axis=1)
    h_ref[...] = h
    z = _rmsnorm(h, g_ref[...]).astype(BF16)
    for k in range(IN_BLOCKS):
        cols = slice(k * D_MODEL, (k + 1) * D_MODEL)
        p_ref[:, cols] = jnp.dot(z, w_ref[:, cols], preferred_element_type=F32).astype(BF16)


def _inproj(x, g, w, tt):
    b, l, _ = x.shape
    ng = b // SUBLANES
    return pl.pallas_call(
        functools.partial(_inproj_kernel, tt=tt),
        grid=(ng, l // tt),
        in_specs=[
            pl.BlockSpec((SUBLANES, tt, D_MODEL), lambda i, t: (i, t, 0)),
            _resident((1, D_MODEL)),
            _resident((D_MODEL, IN_BLOCKS * D_MODEL)),
        ],
        out_specs=[_rows(tt, IN_BLOCKS * D_MODEL), _rows(tt, D_MODEL)],
        out_shape=[
            jax.ShapeDtypeStruct((ng, l * SUBLANES, IN_BLOCKS * D_MODEL), BF16),
            jax.ShapeDtypeStruct((ng, l * SUBLANES, D_MODEL), F32),
        ],
        scratch_shapes=[pltpu.VMEM((N_LANE_SLABS, tt * SUBLANES, LANES), F32)],
        compiler_params=_params(("parallel", "parallel")),
        name="inproj",
    )(x, g, w)


def _s5_kernel(u_ref, w1_ref, ms_ref, c1_ref, c2_ref, d_ref, wglu_ref, s0_ref,
               ya_ref, sfin_ref, st_ref, z_ref, o_ref, y_ref, *, tt):
    ti = pl.program_id(1)

    @pl.when(ti == 0)
    def _():
        st_ref[...] = s0_ref[...]

    nc = tt // SSM_CHUNK
    m = nc * SUBLANES
    chunk_rows = SSM_CHUNK * SUBLANES
    half = SSM_CHUNK // 2
    lane_slot = lax.broadcasted_iota(jnp.int32, (m, LANES), 1) // SSM_GROUP
    in_slot = [lane_slot == s for s in range(GROUPS_PER_SLAB)]

    u = u_ref[...].astype(F32)

    def step_rows(x, t, j):
        return jnp.concatenate(
            [x[c * chunk_rows + t * SUBLANES:c * chunk_rows + (t + 1) * SUBLANES, j * LANES:(j + 1) * LANES]
             for c in range(nc)], axis=0)

    def slab_outputs(j):
        groups = range(j * GROUPS_PER_SLAB, (j + 1) * GROUPS_PER_SLAB)
        gs = slice(j * GROUPS_PER_SLAB, (j + 1) * GROUPS_PER_SLAB)
        rs = [jnp.dot(z_ref[g], w1_ref[g], preferred_element_type=F32) for g in groups]
        s, sw = st_ref[0, gs], st_ref[1, gs]
        c1, c2 = c1_ref[gs], c2_ref[gs]
        carried = []
        for c in range(nc):
            carried.append(s)
            rows_c = slice(c * SUBLANES, (c + 1) * SUBLANES)
            inc = jnp.stack([r[rows_c, SSM_FLAT:SSM_FLAT + SSM_STATE_LANES] for r in rs])
            inc_sw = jnp.stack([r[rows_c, SSM_FLAT + SSM_STATE_LANES:] for r in rs])
            s, sw = c1 * s + c2 * sw + inc, c1 * sw - c2 * s + inc_sw
        st_ref[0, gs] = s
        st_ref[1, gs] = sw
        for i, g in enumerate(groups):
            before = jnp.concatenate([cs[i] for cs in carried], axis=0).astype(BF16)
            o_ref[g] = rs[i][:, :SSM_FLAT] + jnp.dot(before, ms_ref[g], preferred_element_type=F32)

    for j in range(N_LANE_SLABS):
        rot = []
        for t in range(SSM_CHUNK):
            k = t % half
            x = step_rows(u, t, j)
            rot.append(x if k == 0 else pltpu.roll(x, SSM_GROUP * k, axis=1))
        for gg in range(GROUPS_PER_SLAB):
            for hf in range(2):
                acc = rot[half * hf]
                for k in range(1, half):
                    acc = jnp.where(in_slot[(gg + k) % half], rot[half * hf + k], acc)
                z_ref[j * GROUPS_PER_SLAB + gg, :, hf * LANES:(hf + 1) * LANES] = acc.astype(BF16)

    for j in range(N_LANE_SLABS):
        slab_outputs(j)

    for j in range(N_LANE_SLABS):
        for hf in range(2):
            ys = [o_ref[j * GROUPS_PER_SLAB + gg, :, hf * LANES:(hf + 1) * LANES]
                  for gg in range(GROUPS_PER_SLAB)]
            for k in range(half):
                acc = ys[0]
                for gg in range(1, GROUPS_PER_SLAB):
                    acc = jnp.where(in_slot[(gg + k) % half], ys[gg], acc)
                nat = acc if k == 0 else pltpu.roll(acc, LANES - SSM_GROUP * k, axis=1)
                t = half * hf + k
                for c in range(nc):
                    y_ref[c * chunk_rows + t * SUBLANES:c * chunk_rows + (t + 1) * SUBLANES,
                          j * LANES:(j + 1) * LANES] = nat[c * SUBLANES:(c + 1) * SUBLANES]

    y = _gelu_tanh(y_ref[...] + d_ref[...] * u)
    gate = jnp.dot(y.astype(BF16), wglu_ref[...], preferred_element_type=F32)
    ya_ref[...] = (y * _sigmoid(gate)).astype(BF16)

    @pl.when(ti == pl.num_programs(1) - 1)
    def _():
        sfin_ref[...] = st_ref[...]


def _s5(p, w1, ms, c1, c2, d_skip, w_glu, s0, tt):
    ng, rows, _ = p.shape
    m = tt // SSM_CHUNK * SUBLANES
    state_shape = (2, SSM_GROUPS, SUBLANES, SSM_STATE_LANES)
    return pl.pallas_call(
        functools.partial(_s5_kernel, tt=tt),
        grid=(ng, rows // (tt * SUBLANES)),
        in_specs=[
            _rows(tt, D_MODEL, 0),
            _resident(w1.shape),
            _resident(ms.shape),
            _resident(c1.shape),
            _resident(c2.shape),
            _resident((1, D_MODEL)),
            _resident((D_MODEL, D_MODEL)),
            pl.BlockSpec((None,) + state_shape, lambda g, t: (0, 0, 0, 0, 0)),
        ],
        out_specs=[
            _rows(tt, D_MODEL),
            pl.BlockSpec((None,) + state_shape, lambda g, t: (g, 0, 0, 0, 0)),
        ],
        out_shape=[
            jax.ShapeDtypeStruct((ng, rows, D_MODEL), BF16),
            jax.ShapeDtypeStruct((ng,) + state_shape, F32),
        ],
        scratch_shapes=[
            pltpu.VMEM(state_shape, F32),
            pltpu.VMEM((SSM_GROUPS, m, SSM_FLAT), BF16),
            pltpu.VMEM((SSM_GROUPS, m, SSM_FLAT), F32),
            pltpu.VMEM((tt * SUBLANES, D_MODEL), F32),
        ],
        compiler_params=_params(("parallel", "arbitrary")),
        name="s5",
    )(p, w1, ms, c1, c2, d_skip, w_glu, s0)


def _hgrn_kernel(q_ref, f_ref, i_ref, og_ref, lb_ref, ng_ref, s0_ref,
                 yb_ref, sfin_ref, st_ref, qin_ref, kout_ref, v_ref, oint_ref, *, tt, chunk, heads):
    ti = pl.program_id(2)

    @pl.when(ti == 0)
    def _():
        st_ref[...] = s0_ref[...]

    rows = chunk * SUBLANES
    r_idx = lax.broadcasted_iota(jnp.int32, (rows, rows), 0)
    c_idx = lax.broadcasted_iota(jnp.int32, (rows, rows), 1)
    allowed = (((r_idx - c_idx) & (SUBLANES - 1)) == 0) & (c_idx <= r_idx)
    mid = chunk // 2 - 1
    head = (chunk, SUBLANES, HGRN_HEAD_DIM)

    def decays(hh, ci):
        r = slice(ci * rows, (ci + 1) * rows)
        slot = ci % 2
        lanes = slice(hh * HGRN_HEAD_DIM, (hh + 1) * HGRN_HEAD_DIM)
        half_gap = 0.5 * (1.0 - lb_ref[:, lanes])
        swing = half_gap * jnp.tanh(0.5 * f_ref[r, lanes].astype(F32))
        f = (1.0 - half_gap) + swing
        k = (half_gap - swing).reshape(head)
        cum = jnp.log(f).reshape(head)
        shift = 1
        while shift < chunk:
            cum = cum + jnp.concatenate([jnp.zeros((shift,) + head[1:], F32), cum[:-shift]], axis=0)
            shift *= 2
        ref_row = cum[mid:mid + 1]
        last = cum[chunk - 1:chunk]
        q = q_ref[r, lanes].astype(F32).reshape(head)
        q_mid = q * jnp.exp(cum - ref_row)
        k_mid = k * jnp.exp(ref_row - cum)
        qin_ref[slot, hh] = (q_mid * jnp.exp(ref_row)).reshape(rows, HGRN_HEAD_DIM)
        kout_ref[slot, hh] = (k_mid * jnp.exp(last - ref_row)).reshape(rows, HGRN_HEAD_DIM)
        v = i_ref[r, lanes]
        v_ref[slot, hh] = v.astype(F32)
        return (q_mid.reshape(rows, HGRN_HEAD_DIM).astype(BF16),
                k_mid.reshape(rows, HGRN_HEAD_DIM).astype(BF16), v, jnp.exp(last[0]))

    rb =[pl.ds(b, chunk, stride=SUBLANES) for b in range(SUBLANES)]

    def matmuls(ci, operands):
        r = slice(ci * rows, (ci + 1) * rows)
        slot = ci % 2
        q_mid, k_mid, v, decay = zip(*operands)
        scores = [lax.dot_general(q_mid[hh], k_mid[hh], (((1,), (1,)), ((), ())),
                                  preferred_element_type=F32) for hh in range(heads)]
        scores = [jnp.where(allowed, s, 0.0).astype(BF16) for s in scores]
        o = [jnp.dot(scores[hh], v[hh], preferred_element_type=F32) for hh in range(heads)]
        no_keys = jnp.zeros((chunk, HGRN_HEAD_DIM), BF16)
        for b in range(SUBLANES):
            state_t = st_ref[b]
            state_bf = state_t.astype(BF16)
            values, keys = [], []
            for hh in range(heads):
                lanes = slice(hh * HGRN_HEAD_DIM, (hh + 1) * HGRN_HEAD_DIM)
                oint_ref[slot, hh, rb[b], :] = lax.dot_general(
                    qin_ref[slot, hh, rb[b], :].astype(BF16), state_bf[:, lanes],
                    (((1,), (1,)), ((), ())), preferred_element_type=F32)
                values.append(v_ref[slot, hh, rb[b], :].astype(BF16))
                k_out = kout_ref[slot, hh, rb[b], :].astype(BF16)
                keys.append(jnp.concatenate([k_out if other == hh else no_keys for other in range(heads)],
                                            axis=1))
            update = lax.dot_general(jnp.concatenate(values, axis=0), jnp.concatenate(keys, axis=0),
                                     (((0,), (0,)), ((), ())), preferred_element_type=F32)
            decay_b = jnp.concatenate([decay[hh][b:b + 1, :] for hh in range(heads)], axis=1)
            st_ref[b] = state_t * decay_b + update
        for hh in range(heads):
            lanes = slice(hh * HGRN_HEAD_DIM, (hh + 1) * HGRN_HEAD_DIM)
            out = _rmsnorm(o[hh] + oint_ref[slot, hh], ng_ref[:, lanes])
            yb_ref[r, lanes] = (out * _silu(og_ref[r, lanes].astype(F32))).astype(BF16)

    n_chunks = tt // chunk
    operands = [decays(hh, 0) for hh in range(heads)]
    for ci in range(n_chunks):
        following = [decays(hh, ci + 1) for hh in range(heads)] if ci + 1 < n_chunks else None
        matmuls(ci, operands)
        operands = following

    @pl.when(ti == pl.num_programs(2) - 1)
    def _():
        sfin_ref[...] = st_ref[...]


def _hgrn(p, lb, ng_, s0, tt, chunk):
    ng, rows, _ = p.shape
    heads = HGRN_HEADS_PER_STEP
    width = heads * HGRN_HEAD_DIM
    steps = HGRN_HEADS // heads
    col = lambda k: pl.BlockSpec((None, tt * SUBLANES, width), lambda g, h, t, k=k: (g, t, k * steps + h))
    per_head = pl.BlockSpec((1, width), lambda g, h, t: (0, h))
    state_shape = (None, SUBLANES, HGRN_HEAD_DIM, width)
    chunk_buf = pltpu.VMEM((2, heads, chunk * SUBLANES, HGRN_HEAD_DIM), F32)
    return pl.pallas_call(
        functools.partial(_hgrn_kernel, tt=tt, chunk=chunk, heads=heads),
        grid=(ng, steps, rows // (tt * SUBLANES)),
        in_specs=[col(1), col(2), col(3), col(4), per_head, per_head,
                  pl.BlockSpec((None,) + state_shape, lambda g, h, t: (0, h, 0, 0, 0))],
        out_specs=[
            pl.BlockSpec((None, tt * SUBLANES, width), lambda g, h, t: (g, t, h)),
            pl.BlockSpec((None,) + state_shape, lambda g, h, t: (g, h, 0, 0, 0)),
        ],
        out_shape=[
            jax.ShapeDtypeStruct((ng, rows, D_MODEL), BF16),
            jax.ShapeDtypeStruct((ng, steps) + state_shape[1:], F32),
        ],
        scratch_shapes=[pltpu.VMEM(state_shape[1:], F32), chunk_buf, chunk_buf, chunk_buf, chunk_buf],
        compiler_params=_params(("parallel", "parallel", "arbitrary")),
        name="hgrn2",
    )(p, p, p, p, lb, ng_, s0)


def _merge_kernel(h_ref, ga_ref, gb_ref, ya_ref, yb_ref, wa_ref, wb_ref, wo_ref, o_ref):
    a = jnp.dot(ya_ref[...], wa_ref[...], preferred_element_type=F32)
    b = jnp.dot(yb_ref[...], wb_ref[...], preferred_element_type=F32)
    merged = _sigmoid(ga_ref[...].astype(F32)) * a + _sigmoid(gb_ref[...].astype(F32)) * b
    o_ref[...] = h_ref[...] + jnp.dot(merged.astype(BF16), wo_ref[...], preferred_element_type=F32)


def _merge(h, p, ya, yb, wa, wb, wo, tt):
    ng, rows, _ = h.shape
    w = _resident((D_MODEL, D_MODEL))
    return pl.pallas_call(
        _merge_kernel,
        grid=(ng, rows // (tt * SUBLANES)),
        in_specs=[_rows(tt, D_MODEL), _rows(tt, D_MODEL, 5), _rows(tt, D_MODEL, 6),
                  _rows(tt, D_MODEL), _rows(tt, D_MODEL), w, w, w],
        out_specs=_rows(tt, D_MODEL),
        out_shape=jax.ShapeDtypeStruct((ng, rows, D_MODEL), F32),
        compiler_params=_params(("parallel", "parallel")),
        name="merge",
    )(h, p, p, ya, yb, wa, wb, wo)


def _ffn_kernel(h_ref, g_ref, wu_ref, cw_ref, cb_ref, wd_ref, gf_ref,
                c0_ref, o_ref, cfin_ref, carry_ref, act_ref, slab_ref, *, tt):
    ti = pl.program_id(1)

    @pl.when(ti == 0)
    def _():
        carry_ref[...] = c0_ref[...]

    rows = tt * SUBLANES
    tail = (CONV_WIDTH - 1) * SUBLANES
    h = h_ref[...]
    z = _rmsnorm(h, g_ref[...]).astype(BF16)

    def conv_up(n):
        cols = slice(n * FF_CHUNK, (n + 1) * FF_CHUNK)
        pre = jnp.dot(z, wu_ref[:, cols], preferred_element_type=F32)
        ext = jnp.concatenate([carry_ref[n], pre], axis=0)
        carry_ref[n] = ext[rows:rows + tail]
        out = cb_ref[:, cols]
        for j in range(CONV_WIDTH):
            out = out + ext[j * SUBLANES:j * SUBLANES + rows] * cw_ref[j:j + 1, cols]
        return out

    for n in range(N_FF_CHUNKS):
        gate = conv_up(n)
        val = conv_up(N_FF_CHUNKS + n)
        act_ref[:, n * FF_CHUNK:(n + 1) * FF_CHUNK] = (_silu(gate) * val).astype(BF16)

    down = jnp.dot(act_ref[...], wd_ref[...], preferred_element_type=F32)
    out = _rmsnorm(h + down, gf_ref[...])
    for j in range(N_LANE_SLABS):
        slab_ref[j] = out[:, j * LANES:(j + 1) * LANES]
    for b in range(SUBLANES):
        for j in range(N_LANE_SLABS):
            o_ref[b, :, j * LANES:(j + 1) * LANES] = slab_ref[j, pl.ds(b, tt, stride=SUBLANES), :]

    @pl.when(ti == pl.num_programs(1) - 1)
    def _():
        cfin_ref[...] = carry_ref[...]


def _ffn(h, g, wu, cw, cb, wd, gf, c0, tt):
    ng, rows, _ = h.shape
    l = rows // SUBLANES
    carry_shape = (2 * N_FF_CHUNKS, (CONV_WIDTH - 1) * SUBLANES, FF_CHUNK)
    return pl.pallas_call(
        functools.partial(_ffn_kernel, tt=tt),
        grid=(ng, l // tt),
        in_specs=[
            _rows(tt, D_MODEL),
            _resident((1, D_MODEL)),
            _resident(wu.shape), _resident(cw.shape), _resident(cb.shape), _resident(wd.shape),
            _resident((1, D_MODEL)),
            pl.BlockSpec((None,) + carry_shape, lambda g, t: (0, 0, 0, 0)),
        ],
        out_specs=[
            pl.BlockSpec((SUBLANES, tt, D_MODEL), lambda g, t: (g, t, 0)),
            pl.BlockSpec((None,) + carry_shape, lambda g, t: (g, 0, 0, 0)),
        ],
        out_shape=[
            jax.ShapeDtypeStruct((ng * SUBLANES, l, D_MODEL), F32),
            jax.ShapeDtypeStruct((ng,) + carry_shape, F32),
        ],
        scratch_shapes=[
            pltpu.VMEM(carry_shape, F32),
            pltpu.VMEM((tt * SUBLANES, D_FF), BF16),
            pltpu.VMEM((N_LANE_SLABS, tt * SUBLANES, LANES), F32),
        ],
        compiler_params=_params(("parallel", "arbitrary")),
        name="ffn",
    )(h, g, wu, cw, cb, wd, gf, c0)


def _s5_params(lam_re, lam_im, log_dt, b_re, b_im, c_re, c_im):
    hp = lax.Precision.HIGHEST
    t_len = SSM_CHUNK
    dt = jnp.exp(log_dt)[:, None]
    mag = jnp.exp(lam_re * dt)
    ab_re = mag * jnp.cos(lam_im * dt)
    ab_im = mag * jnp.sin(lam_im * dt)
    den = lam_re * lam_re + lam_im * lam_im
    nr = ab_re - 1.0
    coef_re = (nr * lam_re + ab_im * lam_im) / den
    coef_im = (ab_im * lam_re - nr * lam_im) / den
    bb_re = coef_re[..., None] * b_re - coef_im[..., None] * b_im
    bb_im = coef_re[..., None] * b_im + coef_im[..., None] * b_re
    tau = jnp.arange(t_len + 1, dtype=F32)[None, :, None]
    pw_mag = jnp.exp((lam_re * dt)[:, None, :] * tau)
    pw_re = pw_mag * jnp.cos((lam_im * dt)[:, None, :] * tau)
    pw_im = pw_mag * jnp.sin((lam_im * dt)[:, None, :] * tau)
    ab_b_re = pw_re[..., None] * bb_re[:, None] - pw_im[..., None] * bb_im[:, None]
    ab_b_im = pw_re[..., None] * bb_im[:, None] + pw_im[..., None] * bb_re[:, None]
    kern = (jnp.einsum("gop,gtpi->gtoi", c_re, ab_b_re[:, :t_len], precision=hp)
            - jnp.einsum("gop,gtpi->gtoi", c_im, ab_b_im[:, :t_len], precision=hp))
    out_re = c_re[:, None] * pw_re[:, :, None, :] - c_im[:, None] * pw_im[:, :, None, :]
    out_im = -(c_re[:, None] * pw_im[:, :, None, :] + c_im[:, None] * pw_re[:, :, None, :])

    half = t_len // 2
    slots = np.arange(t_len)
    step_of = np.stack([half * (slots // half) + (slots % half - gg) % half
                        for gg in range(GROUPS_PER_SLAB)])
    lags = np.arange(t_len + 1)
    inc_sel = (t_len - 1 - step_of[:, :, None]) == lags[None, None, :t_len]
    out_sel = (step_of[:, :, None] + 1) == lags[None, None, :]
    by_slab = lambda a: a.astype(BF16).reshape((N_LANE_SLABS, GROUPS_PER_SLAB) + a.shape[1:])
    select = lambda spec, sel, a: jnp.einsum(spec, jnp.asarray(sel, BF16), by_slab(a),
                                             preferred_element_type=F32).astype(BF16)
    dims = (GROUPS_PER_SLAB, SSM_FLAT, t_len, SSM_FLAT)
    x_i, k_i, r_i, c_i = (lax.broadcasted_iota(jnp.int32, dims, d) for d in range(4))
    step_at = lambda s: half * (s // half) + (s % half - x_i) % half
    lag_eye = ((step_at(c_i // SSM_GROUP) - step_at(r_i) == k_i // SSM_GROUP)
               & (k_i % SSM_GROUP == c_i % SSM_GROUP)).astype(BF16)
    kern_rows = by_slab(kern.transpose(0, 3, 1, 2).reshape(SSM_GROUPS, SSM_GROUP, SSM_FLAT))
    toep = jnp.einsum("jxik,xkrC->jxriC", kern_rows, lag_eye, preferred_element_type=F32)
    toep = toep.astype(BF16).reshape(SSM_GROUPS, SSM_FLAT, SSM_FLAT)
    inc_re = select("xsu,jxuph->jxshp", inc_sel, ab_b_re[:, :t_len])
    inc_im = select("xsu,jxuph->jxshp", inc_sel, ab_b_im[:, :t_len])
    inc = jnp.concatenate([inc_re, inc_im, inc_im, inc_re], axis=-1)
    w1 = jnp.concatenate([toep, inc.reshape(SSM_GROUPS, SSM_FLAT, 2 * SSM_STATE_LANES)], axis=-1)
    ms = jnp.concatenate([select("xsu,jxuop->jxpso", out_sel, out_re),
                          select("xsu,jxuop->jxpso", out_sel, out_im)], axis=2)
    ms = ms.reshape(SSM_GROUPS, SSM_STATE_LANES, SSM_FLAT)
    a16_re, a16_im = pw_re[:, t_len], pw_im[:, t_len]
    c1 = jnp.concatenate([a16_re, a16_re], axis=-1)
    c2 = jnp.concatenate([-a16_im, a16_im], axis=-1)
    bcast = lambda a: jnp.broadcast_to(a[:, None, :], (SSM_GROUPS, SUBLANES, SSM_STATE_LANES))
    return w1, ms, bcast(c1), bcast(c2)


def _block(x, states, w):
    l = x.shape[1]
    s5_state, hgrn_state, conv_carry = states
    t_in, t_s5, t_hgrn, t_merge, t_ffn = (min(t, l) for t in TILES)
    p, h = _inproj(x, w["mix_g"], w["w_in"], t_in)
    ya, s5_state = _s5(p, w["s5_w1"], w["s5_ms"], w["s5_c1"], w["s5_c2"], w["d_skip"], w["w_glu"],
                       s5_state, t_s5)
    yb, hgrn_state = _hgrn(p, w["lb"], w["hgrn_g"], hgrn_state, t_hgrn, min(HGRN_CHUNK, l))
    h1 = _merge(h, p, ya, yb, w["w_a"], w["w_b"], w["w_out"], t_merge)
    out, conv_carry = _ffn(h1, w["ffn_g"], w["w_up"], w["conv_w"], w["conv_b"], w["w_down"],
                           w["final_g"], conv_carry, t_ffn)
    return out, (s5_state, hgrn_state, conv_carry)


def kernel(x, meta_tokens, mix_norm_g, w_in, ssm_lambda_re, ssm_lambda_im, ssm_log_dt, ssm_b_re, ssm_b_im, ssm_c_re, ssm_c_im, ssm_d, ssm_w_glu, w_ssm_proj, hgrn_lb_logits, hgrn_norm_g, w_hgrn_proj, w_out, ffn_norm_g, w_up, conv_w, conv_b, w_down, final_norm_g):
    assert x.shape[0] % SUBLANES == 0 and x.shape[1] % SSM_CHUNK == 0 and x.shape[2] == D_MODEL
    s5_w1, s5_ms, s5_c1, s5_c2 = _s5_params(ssm_lambda_re[0], ssm_lambda_im[0], ssm_log_dt[0],
                                            ssm_b_re[0], ssm_b_im[0], ssm_c_re[0], ssm_c_im[0])
    lower_bounds = jnp.cumsum(jax.nn.softmax(hgrn_lb_logits.astype(F32), axis=0), axis=0)
    w = {
        "mix_g": mix_norm_g[0].reshape(1, D_MODEL),
        "w_in": w_in[0].astype(BF16),
        "s5_w1": s5_w1, "s5_ms": s5_ms, "s5_c1": s5_c1, "s5_c2": s5_c2,
        "d_skip": ssm_d[0].reshape(1, D_MODEL),
        "w_glu": ssm_w_glu[0].astype(BF16),
        "lb": lower_bounds[0].reshape(1, D_MODEL),
        "hgrn_g": jnp.tile(hgrn_norm_g[0], HGRN_HEADS).reshape(1, D_MODEL),
        "w_a": w_ssm_proj[0].astype(BF16),
        "w_b": w_hgrn_proj[0].astype(BF16),
        "w_out": w_out[0].astype(BF16),
        "ffn_g": ffn_norm_g[0].reshape(1, D_MODEL),
        "w_up": w_up[0].astype(BF16),
        "conv_w": conv_w[0],
        "conv_b": conv_b[0].reshape(1, 2 * D_FF),
        "w_down": w_down[0].astype(BF16),
        "final_g": final_norm_g.reshape(1, D_MODEL),
    }
    zero_states = (
        jnp.zeros((1, 2, SSM_GROUPS, SUBLANES, SSM_STATE_LANES), F32),
        jnp.zeros((1, HGRN_HEADS // HGRN_HEADS_PER_STEP, SUBLANES, HGRN_HEAD_DIM,
                   HGRN_HEADS_PER_STEP * HGRN_HEAD_DIM), F32),
        jnp.zeros((1, 2 * N_FF_CHUNKS, (CONV_WIDTH - 1) * SUBLANES, FF_CHUNK), F32),
    )
    meta = jnp.broadcast_to(meta_tokens.astype(x.dtype)[None], (SUBLANES, N_META, D_MODEL))
    _, states = _block(meta, zero_states, w)
    out, _ = _block(x, states, w)
    return out
```

```python
import functools
import math

import numpy as np
import jax
import jax.numpy as jnp
from jax import lax
from jax.experimental import pallas as pl
from jax.experimental.pallas import tpu as pltpu

F32 = jnp.float32
BF16 = jnp.bfloat16

D_MODEL = 1024
N_META = 16
SSM_GROUPS = 64
SSM_GROUP = 16
SSM_STATE = 64
SSM_CHUNK = 16
SSM_FLAT = SSM_CHUNK * SSM_GROUP
SSM_STATE_LANES = 2 * SSM_STATE
HGRN_HEADS = 8
HGRN_HEAD_DIM = 128
HGRN_CHUNK = 32
HGRN_HEADS_PER_STEP = 2
D_FF = 2816
FF_CHUNK = 256
N_FF_CHUNKS = D_FF // FF_CHUNK
CONV_WIDTH = 3
IN_BLOCKS = 7
EPS = 1e-6
GELU_C0 = math.sqrt(2.0 / math.pi)
GELU_C1 = 0.044715 * GELU_C0
LANES = 128
SUBLANES = 8
N_LANE_SLABS = D_MODEL // LANES
GROUPS_PER_SLAB = LANES // SSM_GROUP
VMEM_LIMIT_BYTES = 56 * 1024 * 1024
TILES = (64, 128, 512, 128, 128)


def _rmsnorm(x, g):
    return x * lax.rsqrt(jnp.mean(x * x, axis=-1, keepdims=True) + EPS) * g


def _sigmoid(x):
    return 0.5 * jnp.tanh(0.5 * x) + 0.5


def _silu(x):
    h = 0.5 * x
    return h + h * jnp.tanh(h)


def _gelu_tanh(x):
    h = 0.5 * x
    return h + h * jnp.tanh(x * (GELU_C0 + GELU_C1 * (x * x)))


def _resident(shape):
    n = len(shape)
    return pl.BlockSpec(shape, lambda *_: (0,) * n, pipeline_mode=pl.Buffered(1))


def _params(semantics):
    return pltpu.CompilerParams(dimension_semantics=semantics, vmem_limit_bytes=VMEM_LIMIT_BYTES)


def _rows(tt, cols, k=0):
    return pl.BlockSpec((None, tt * SUBLANES, cols), lambda g, t, k=k: (g, t, k))


def _inproj_kernel(x_ref, g_ref, w_ref, p_ref, h_ref, slab_ref, *, tt):
    for b in range(SUBLANES):
        for j in range(N_LANE_SLABS):
            slab_ref[j, pl.ds(b, tt, stride=SUBLANES), :] = x_ref[b, :, j * LANES:(j + 1) * LANES]
    h = jnp.concatenate([slab_ref[j] for j in range(N_LANE_SLABS)], axis=1)
    h_ref[...] = h
    z = _rmsnorm(h, g_ref[...]).astype(BF16)
    for k in range(IN_BLOCKS):
        cols = slice(k * D_MODEL, (k + 1) * D_MODEL)
        p_ref[:, cols] = jnp.dot(z, w_ref[:, cols], preferred_element_type=F32).astype(BF16)


def _inproj(x, g, w, tt):
    b, l, _ = x.shape
    ng = b // SUBLANES
    return pl.pallas_call(
        functools.partial(_inproj_kernel, tt=tt),
        grid=(ng, l // tt),
        in_specs=[
            pl.BlockSpec((SUBLANES, tt, D_MODEL), lambda i, t: (i, t, 0)),
            _resident((1, D_MODEL)),
            _resident((D_MODEL, IN_BLOCKS * D_MODEL)),
        ],
        out_specs=[_rows(tt, IN_BLOCKS * D_MODEL), _rows(tt, D_MODEL)],
        out_shape=[
            jax.ShapeDtypeStruct((ng, l * SUBLANES, IN_BLOCKS * D_MODEL), BF16),
            jax.ShapeDtypeStruct((ng, l * SUBLANES, D_MODEL), F32),
        ],
        scratch_shapes=[pltpu.VMEM((N_LANE_SLABS, tt * SUBLANES, LANES), F32)],
        compiler_params=_params(("parallel", "parallel")),
        name="inproj",
    )(x, g, w)


def _s5_kernel(u_ref, w1_ref, ms_ref, c1_ref, c2_ref, d_ref, wglu_ref, s0_ref,
               ya_ref, sfin_ref, st_ref, z_ref, o_ref, y_ref, *, tt):
    ti = pl.program_id(1)

    @pl.when(ti == 0)
    def _():
        st_ref[...] = s0_ref[...]

    nc = tt // SSM_CHUNK
    m = nc * SUBLANES
    chunk_rows = SSM_CHUNK * SUBLANES
    half = SSM_CHUNK // 2
    lane_slot = lax.broadcasted_iota(jnp.int32, (m, LANES), 1) // SSM_GROUP
    in_slot = [lane_slot == s for s in range(GROUPS_PER_SLAB)]

    u = u_ref[...].astype(F32)

    def step_rows(x, t, j):
        return jnp.concatenate(
            [x[c * chunk_rows + t * SUBLANES:c * chunk_rows + (t + 1) * SUBLANES, j * LANES:(j + 1) * LANES]
             for c in range(nc)], axis=0)

    def slab_outputs(j):
        groups = range(j * GROUPS_PER_SLAB, (j + 1) * GROUPS_PER_SLAB)
        gs = slice(j * GROUPS_PER_SLAB, (j + 1) * GROUPS_PER_SLAB)
        rs = [jnp.dot(z_ref[g], w1_ref[g], preferred_element_type=F32) for g in groups]
        swapped = [pltpu.roll(r[:, SSM_FLAT:], SSM_STATE, axis=1) for r in rs]
        s, sw = st_ref[0, gs], st_ref[1, gs]
        c1, c2 = c1_ref[gs], c2_ref[gs]
        carried = []
        for c in range(nc):
            carried.append(s)
            rows_c = slice(c * SUBLANES, (c + 1) * SUBLANES)
            inc = jnp.stack([r[rows_c, SSM_FLAT:] for r in rs])
            inc_sw = jnp.stack([r[rows_c] for r in swapped])
            s, sw = c1 * s + c2 * sw + inc, c1 * sw - c2 * s + inc_sw
        st_ref[0, gs] = s
        st_ref[1, gs] = sw
        for i, g in enumerate(groups):
            before = jnp.concatenate([cs[i] for cs in carried], axis=0).astype(BF16)
            o_ref[g] = rs[i][:, :SSM_FLAT] + jnp.dot(before, ms_ref[g], preferred_element_type=F32)

    for j in range(N_LANE_SLABS):
        rot = []
        for t in range(SSM_CHUNK):
            k = t % half
            x = step_rows(u, t, j)
            rot.append(x if k == 0 else pltpu.roll(x, SSM_GROUP * k, axis=1))
        for gg in range(GROUPS_PER_SLAB):
            for hf in range(2):
                acc = rot[half * hf]
                for k in range(1, half):
                    acc = jnp.where(in_slot[(gg + k) % half], rot[half * hf + k], acc)
                z_ref[j * GROUPS_PER_SLAB + gg, :, hf * LANES:(hf + 1) * LANES] = acc.astype(BF16)

    for j in range(N_LANE_SLABS):
        slab_outputs(j)

    for j in range(N_LANE_SLABS):
        for hf in range(2):
            ys = [o_ref[j * GROUPS_PER_SLAB + gg, :, hf * LANES:(hf + 1) * LANES]
                  for gg in range(GROUPS_PER_SLAB)]
            for k in range(half):
                acc = ys[0]
                for gg in range(1, GROUPS_PER_SLAB):
                    acc = jnp.where(in_slot[(gg + k) % half], ys[gg], acc)
                nat = acc if k == 0 else pltpu.roll(acc, LANES - SSM_GROUP * k, axis=1)
                t = half * hf + k
                for c in range(nc):
                    y_ref[c * chunk_rows + t * SUBLANES:c * chunk_rows + (t + 1) * SUBLANES,
                          j * LANES:(j + 1) * LANES] = nat[c * SUBLANES:(c + 1) * SUBLANES]

    y = _gelu_tanh(y_ref[...] + d_ref[...] * u)
    gate = jnp.dot(y.astype(BF16), wglu_ref[...], preferred_element_type=F32)
    ya_ref[...] = (y * _sigmoid(gate)).astype(BF16)

    @pl.when(ti == pl.num_programs(1) - 1)
    def _():
        sfin_ref[...] = st_ref[...]


def _s5(p, w1, ms, c1, c2, d_skip, w_glu, s0, tt):
    ng, rows, _ = p.shape
    m = tt // SSM_CHUNK * SUBLANES
    state_shape = (2, SSM_GROUPS, SUBLANES, SSM_STATE_LANES)
    return pl.pallas_call(
        functools.partial(_s5_kernel, tt=tt),
        grid=(ng, rows // (tt * SUBLANES)),
        in_specs=[
            _rows(tt, D_MODEL, 0),
            _resident(w1.shape),
            _resident(ms.shape),
            _resident(c1.shape),
            _resident(c2.shape),
            _resident((1, D_MODEL)),
            _resident((D_MODEL, D_MODEL)),
            pl.BlockSpec((None,) + state_shape, lambda g, t: (0, 0, 0, 0, 0)),
        ],
        out_specs=[
            _rows(tt, D_MODEL),
            pl.BlockSpec((None,) + state_shape, lambda g, t: (g, 0, 0, 0, 0)),
        ],
        out_shape=[
            jax.ShapeDtypeStruct((ng, rows, D_MODEL), BF16),
            jax.ShapeDtypeStruct((ng,) + state_shape, F32),
        ],
        scratch_shapes=[
            pltpu.VMEM(state_shape, F32),
            pltpu.VMEM((SSM_GROUPS, m, SSM_FLAT), BF16),
            pltpu.VMEM((SSM_GROUPS, m, SSM_FLAT), F32),
            pltpu.VMEM((tt * SUBLANES, D_MODEL), F32),
        ],
        compiler_params=_params(("parallel", "arbitrary")),
        name="s5",
    )(p, w1, ms, c1, c2, d_skip, w_glu, s0)


def _hgrn_kernel(q_ref, f_ref, i_ref, og_ref, lb_ref, ng_ref, s0_ref,
                 yb_ref, sfin_ref, st_ref, qin_ref, kout_ref, v_ref, oint_ref, *, tt, chunk, heads):
    ti = pl.program_id(2)

    @pl.when(ti == 0)
    def _():
        st_ref[...] = s0_ref[...]

    rows = chunk * SUBLANES
    r_idx = lax.broadcasted_iota(jnp.int32, (rows, rows), 0)
    c_idx = lax.broadcasted_iota(jnp.int32, (rows, rows), 1)
    allowed = (((r_idx - c_idx) & (SUBLANES - 1)) == 0) & (c_idx <= r_idx)
    mid = chunk // 2 - 1
    head = (chunk, SUBLANES, HGRN_HEAD_DIM)

    def decays(hh, ci):
        r = slice(ci * rows, (ci + 1) * rows)
        slot = ci % 2
        lanes = slice(hh * HGRN_HEAD_DIM, (hh + 1) * HGRN_HEAD_DIM)
        half_gap = 0.5 * (1.0 - lb_ref[:, lanes])
        swing = half_gap * jnp.tanh(0.5 * f_ref[r, lanes].astype(F32))
        f = (1.0 - half_gap) + swing
        k = (half_gap - swing).reshape(head)
        cum = jnp.log(f).reshape(head)
        shift = 1
        while shift < chunk:
            cum = cum + jnp.concatenate([jnp.zeros((shift,) + head[1:], F32), cum[:-shift]], axis=0)
            shift *= 2
        ref_row = cum[mid:mid + 1]
        last = cum[chunk - 1:chunk]
        q = q_ref[r, lanes].astype(F32).reshape(head)
        q_mid = q * jnp.exp(cum - ref_row)
        k_mid = k * jnp.exp(ref_row - cum)
        qin_ref[slot, hh] = (q_mid * jnp.exp(ref_row)).reshape(rows, HGRN_HEAD_DIM)
        kout_ref[slot, hh] = (k_mid * jnp.exp(last - ref_row)).reshape(rows, HGRN_HEAD_DIM)
        v = i_ref[r, lanes]
        v_ref[slot, hh] = v.astype(F32)
        return (q_mid.reshape(rows, HGRN_HEAD_DIM).astype(BF16),
                k_mid.reshape(rows, HGRN_HEAD_DIM).astype(BF16), v, jnp.exp(last[0]))

    rb =[pl.ds(b, chunk, stride=SUBLANES) for b in range(SUBLANES)]

    def matmuls(ci, operands):
        r = slice(ci * rows, (ci + 1) * rows)
        slot = ci % 2
        q_mid, k_mid, v, decay = zip(*operands)
        scores = [lax.dot_general(q_mid[hh], k_mid[hh], (((1,), (1,)), ((), ())),
                                  preferred_element_type=F32) for hh in range(heads)]
        scores = [jnp.where(allowed, s, 0.0).astype(BF16) for s in scores]
        o = [jnp.dot(scores[hh], v[hh], preferred_element_type=F32) for hh in range(heads)]
        no_keys = jnp.zeros((chunk, HGRN_HEAD_DIM), BF16)
        for b in range(SUBLANES):
            state_t = st_ref[b]
            state_bf = state_t.astype(BF16)
            values, keys = [], []
            for hh in range(heads):
                lanes = slice(hh * HGRN_HEAD_DIM, (hh + 1) * HGRN_HEAD_DIM)
                oint_ref[slot, hh, rb[b], :] = lax.dot_general(
                    qin_ref[slot, hh, rb[b], :].astype(BF16), state_bf[:, lanes],
                    (((1,), (1,)), ((), ())), preferred_element_type=F32)
                values.append(v_ref[slot, hh, rb[b], :].astype(BF16))
                k_out = kout_ref[slot, hh, rb[b], :].astype(BF16)
                keys.append(jnp.concatenate([k_out if other == hh else no_keys for other in range(heads)],
                                            axis=1))
            update = lax.dot_general(jnp.concatenate(values, axis=0), jnp.concatenate(keys, axis=0),
                                     (((0,), (0,)), ((), ())), preferred_element_type=F32)
            decay_b = jnp.concatenate([decay[hh][b:b + 1, :] for hh in range(heads)], axis=1)
            st_ref[b] = state_t * decay_b + update
        for hh in range(heads):
            lanes = slice(hh * HGRN_HEAD_DIM, (hh + 1) * HGRN_HEAD_DIM)
            out = _rmsnorm(o[hh] + oint_ref[slot, hh], ng_ref[:, lanes])
            yb_ref[r, lanes] = (out * _silu(og_ref[r, lanes].astype(F32))).astype(BF16)

    n_chunks = tt // chunk
    operands = [decays(hh, 0) for hh in range(heads)]
    for ci in range(n_chunks):
        following = [decays(hh, ci + 1) for hh in range(heads)] if ci + 1 < n_chunks else None
        matmuls(ci, operands)
        operands = following

    @pl.when(ti == pl.num_programs(2) - 1)
    def _():
        sfin_ref[...] = st_ref[...]


def _hgrn(p, lb, ng_, s0, tt, chunk):
    ng, rows, _ = p.shape
    heads = HGRN_HEADS_PER_STEP
    width = heads * HGRN_HEAD_DIM
    steps = HGRN_HEADS // heads
    col = lambda k: pl.BlockSpec((None, tt * SUBLANES, width), lambda g, h, t, k=k: (g, t, k * steps + h))
    per_head = pl.BlockSpec((1, width), lambda g, h, t: (0, h))
    state_shape = (None, SUBLANES, HGRN_HEAD_DIM, width)
    chunk_buf = pltpu.VMEM((2, heads, chunk * SUBLANES, HGRN_HEAD_DIM), F32)
    return pl.pallas_call(
        functools.partial(_hgrn_kernel, tt=tt, chunk=chunk, heads=heads),
        grid=(ng, steps, rows // (tt * SUBLANES)),
        in_specs=[col(1), col(2), col(3), col(4), per_head, per_head,
                  pl.BlockSpec((None,) + state_shape, lambda g, h, t: (0, h, 0, 0, 0))],
        out_specs=[
            pl.BlockSpec((None, tt * SUBLANES, width), lambda g, h, t: (g, t, h)),
            pl.BlockSpec((None,) + state_shape, lambda g, h, t: (g, h, 0, 0, 0)),
        ],
        out_shape=[
            jax.ShapeDtypeStruct((ng, rows, D_MODEL), BF16),
            jax.ShapeDtypeStruct((ng, steps) + state_shape[1:], F32),
        ],
        scratch_shapes=[pltpu.VMEM(state_shape[1:], F32), chunk_buf, chunk_buf, chunk_buf, chunk_buf],
        compiler_params=_params(("parallel", "parallel", "arbitrary")),
        name="hgrn2",
    )(p, p, p, p, lb, ng_, s0)


def _merge_kernel(h_ref, ga_ref, gb_ref, ya_ref, yb_ref, wa_ref, wb_ref, wo_ref, o_ref):
    a = jnp.dot(ya_ref[...], wa_ref[...], preferred_element_type=F32)
    b = jnp.dot(yb_ref[...], wb_ref[...], preferred_element_type=F32)
    merged = _sigmoid(ga_ref[...].astype(F32)) * a + _sigmoid(gb_ref[...].astype(F32)) * b
    o_ref[...] = h_ref[...] + jnp.dot(merged.astype(BF16), wo_ref[...], preferred_element_type=F32)


def _merge(h, p, ya, yb, wa, wb, wo, tt):
    ng, rows, _ = h.shape
    w = _resident((D_MODEL, D_MODEL))
    return pl.pallas_call(
        _merge_kernel,
        grid=(ng, rows // (tt * SUBLANES)),
        in_specs=[_rows(tt, D_MODEL), _rows(tt, D_MODEL, 5), _rows(tt, D_MODEL, 6),
                  _rows(tt, D_MODEL), _rows(tt, D_MODEL), w, w, w],
        out_specs=_rows(tt, D_MODEL),
        out_shape=jax.ShapeDtypeStruct((ng, rows, D_MODEL), F32),
        compiler_params=_params(("parallel", "parallel")),
        name="merge",
    )(h, p, p, ya, yb, wa, wb, wo)


def _ffn_kernel(h_ref, g_ref, wu_ref, cw_ref, cb_ref, wd_ref, gf_ref,
                c0_ref, o_ref, cfin_ref, carry_ref, act_ref, slab_ref, *, tt):
    ti = pl.program_id(1)

    @pl.when(ti == 0)
    def _():
        carry_ref[...] = c0_ref[...]

    rows = tt * SUBLANES
    tail = (CONV_WIDTH - 1) * SUBLANES
    h = h_ref[...]
    z = _rmsnorm(h, g_ref[...]).astype(BF16)

    def conv_up(n):
        cols = slice(n * FF_CHUNK, (n + 1) * FF_CHUNK)
        pre = jnp.dot(z, wu_ref[:, cols], preferred_element_type=F32)
        ext = jnp.concatenate([carry_ref[n], pre], axis=0)
        carry_ref[n] = ext[rows:rows + tail]
        out = cb_ref[:, cols]
        for j in range(CONV_WIDTH):
            out = out + ext[j * SUBLANES:j * SUBLANES + rows] * cw_ref[j:j + 1, cols]
        return out

    for n in range(N_FF_CHUNKS):
        gate = conv_up(n)
        val = conv_up(N_FF_CHUNKS + n)
        act_ref[:, n * FF_CHUNK:(n + 1) * FF_CHUNK] = (_silu(gate) * val).astype(BF16)

    down = jnp.dot(act_ref[...], wd_ref[...], preferred_element_type=F32)
    out = _rmsnorm(h + down, gf_ref[...])
    for j in range(N_LANE_SLABS):
        slab_ref[j] = out[:, j * LANES:(j + 1) * LANES]
    for b in range(SUBLANES):
        for j in range(N_LANE_SLABS):
            o_ref[b, :, j * LANES:(j + 1) * LANES] = slab_ref[j, pl.ds(b, tt, stride=SUBLANES), :]

    @pl.when(ti == pl.num_programs(1) - 1)
    def _():
        cfin_ref[...] = carry_ref[...]


def _ffn(h, g, wu, cw, cb, wd, gf, c0, tt):
    ng, rows, _ = h.shape
    l = rows // SUBLANES
    carry_shape = (2 * N_FF_CHUNKS, (CONV_WIDTH - 1) * SUBLANES, FF_CHUNK)
    return pl.pallas_call(
        functools.partial(_ffn_kernel, tt=tt),
        grid=(ng, l // tt),
        in_specs=[
            _rows(tt, D_MODEL),
            _resident((1, D_MODEL)),
            _resident(wu.shape), _resident(cw.shape), _resident(cb.shape), _resident(wd.shape),
            _resident((1, D_MODEL)),
            pl.BlockSpec((None,) + carry_shape, lambda g, t: (0, 0, 0, 0)),
        ],
        out_specs=[
            pl.BlockSpec((SUBLANES, tt, D_MODEL), lambda g, t: (g, t, 0)),
            pl.BlockSpec((None,) + carry_shape, lambda g, t: (g, 0, 0, 0)),
        ],
        out_shape=[
            jax.ShapeDtypeStruct((ng * SUBLANES, l, D_MODEL), F32),
            jax.ShapeDtypeStruct((ng,) + carry_shape, F32),
        ],
        scratch_shapes=[
            pltpu.VMEM(carry_shape, F32),
            pltpu.VMEM((tt * SUBLANES, D_FF), BF16),
            pltpu.VMEM((N_LANE_SLABS, tt * SUBLANES, LANES), F32),
        ],
        compiler_params=_params(("parallel", "arbitrary")),
        name="ffn",
    )(h, g, wu, cw, cb, wd, gf, c0)


def _s5_params(lam_re, lam_im, log_dt, b_re, b_im, c_re, c_im):
    hp = lax.Precision.HIGHEST
    t_len = SSM_CHUNK
    dt = jnp.exp(log_dt)[:, None]
    mag = jnp.exp(lam_re * dt)
    ab_re = mag * jnp.cos(lam_im * dt)
    ab_im = mag * jnp.sin(lam_im * dt)
    den = lam_re * lam_re + lam_im * lam_im
    nr = ab_re - 1.0
    coef_re = (nr * lam_re + ab_im * lam_im) / den
    coef_im = (ab_im * lam_re - nr * lam_im) / den
    bb_re = coef_re[..., None] * b_re - coef_im[..., None] * b_im
    bb_im = coef_re[..., None] * b_im + coef_im[..., None] * b_re
    tau = jnp.arange(t_len + 1, dtype=F32)[None, :, None]
    pw_mag = jnp.exp((lam_re * dt)[:, None, :] * tau)
    pw_re = pw_mag * jnp.cos((lam_im * dt)[:, None, :] * tau)
    pw_im = pw_mag * jnp.sin((lam_im * dt)[:, None, :] * tau)
    ab_b_re = pw_re[..., None] * bb_re[:, None] - pw_im[..., None] * bb_im[:, None]
    ab_b_im = pw_re[..., None] * bb_im[:, None] + pw_im[..., None] * bb_re[:, None]
    kern = (jnp.einsum("gop,gtpi->gtoi", c_re, ab_b_re[:, :t_len], precision=hp)
            - jnp.einsum("gop,gtpi->gtoi", c_im, ab_b_im[:, :t_len], precision=hp))
    out_re = c_re[:, None] * pw_re[:, :, None, :] - c_im[:, None] * pw_im[:, :, None, :]
    out_im = -(c_re[:, None] * pw_im[:, :, None, :] + c_im[:, None] * pw_re[:, :, None, :])

    half = t_len // 2
    slots = np.arange(t_len)
    step_of = np.stack([half * (slots // half) + (slots % half - gg) % half
                        for gg in range(GROUPS_PER_SLAB)])
    lags = np.arange(t_len + 1)
    inc_sel = (t_len - 1 - step_of[:, :, None]) == lags[None, None, :t_len]
    out_sel = (step_of[:, :, None] + 1) == lags[None, None, :]
    by_slab = lambda a: a.astype(BF16).reshape((N_LANE_SLABS, GROUPS_PER_SLAB) + a.shape[1:])
    select = lambda spec, sel, a: jnp.einsum(spec, jnp.asarray(sel, BF16), by_slab(a),
                                             preferred_element_type=F32).astype(BF16)
    dims = (GROUPS_PER_SLAB, SSM_FLAT, t_len, SSM_FLAT)
    x_i, k_i, r_i, c_i = (lax.broadcasted_iota(jnp.int32, dims, d) for d in range(4))
    step_at = lambda s: half * (s // half) + (s % half - x_i) % half
    lag_eye = ((step_at(c_i // SSM_GROUP) - step_at(r_i) == k_i // SSM_GROUP)
               & (k_i % SSM_GROUP == c_i % SSM_GROUP)).astype(BF16)
    kern_rows = by_slab(kern.transpose(0, 3, 1, 2).reshape(SSM_GROUPS, SSM_GROUP, SSM_FLAT))
    toep = jnp.einsum("jxik,xkrC->jxriC", kern_rows, lag_eye, preferred_element_type=F32)
    toep = toep.astype(BF16).reshape(SSM_GROUPS, SSM_FLAT, SSM_FLAT)
    inc_re = select("xsu,jxuph->jxshp", inc_sel, ab_b_re[:, :t_len])
    inc_im = select("xsu,jxuph->jxshp", inc_sel, ab_b_im[:, :t_len])
    inc = jnp.concatenate([inc_re, inc_im], axis=-1)
    w1 = jnp.concatenate([toep, inc.reshape(SSM_GROUPS, SSM_FLAT, SSM_STATE_LANES)], axis=-1)
    ms = jnp.concatenate([select("xsu,jxuop->jxpso", out_sel, out_re),
                          select("xsu,jxuop->jxpso", out_sel, out_im)], axis=2)
    ms = ms.reshape(SSM_GROUPS, SSM_STATE_LANES, SSM_FLAT)
    a16_re, a16_im = pw_re[:, t_len], pw_im[:, t_len]
    c1 = jnp.concatenate([a16_re, a16_re], axis=-1)
    c2 = jnp.concatenate([-a16_im, a16_im], axis=-1)
    bcast = lambda a: jnp.broadcast_to(a[:, None, :], (SSM_GROUPS, SUBLANES, SSM_STATE_LANES))
    return w1, ms, bcast(c1), bcast(c2)


def _block(x, states, w):
    l = x.shape[1]
    s5_state, hgrn_state, conv_carry = states
    t_in, t_s5, t_hgrn, t_merge, t_ffn = (min(t, l) for t in TILES)
    p, h = _inproj(x, w["mix_g"], w["w_in"], t_in)
    ya, s5_state = _s5(p, w["s5_w1"], w["s5_ms"], w["s5_c1"], w["s5_c2"], w["d_skip"], w["w_glu"],
                       s5_state, t_s5)
    yb, hgrn_state = _hgrn(p, w["lb"], w["hgrn_g"], hgrn_state, t_hgrn, min(HGRN_CHUNK, l))
    h1 = _merge(h, p, ya, yb, w["w_a"], w["w_b"], w["w_out"], t_merge)
    out, conv_carry = _ffn(h1, w["ffn_g"], w["w_up"], w["conv_w"], w["conv_b"], w["w_down"],
                           w["final_g"], conv_carry, t_ffn)
    return out, (s5_state, hgrn_state, conv_carry)


def kernel(x, meta_tokens, mix_norm_g, w_in, ssm_lambda_re, ssm_lambda_im, ssm_log_dt, ssm_b_re, ssm_b_im, ssm_c_re, ssm_c_im, ssm_d, ssm_w_glu, w_ssm_proj, hgrn_lb_logits, hgrn_norm_g, w_hgrn_proj, w_out, ffn_norm_g, w_up, conv_w, conv_b, w_down, final_norm_g):
    assert x.shape[0] % SUBLANES == 0 and x.shape[1] % SSM_CHUNK == 0 and x.shape[2] == D_MODEL
    s5_w1, s5_ms, s5_c1, s5_c2 = _s5_params(ssm_lambda_re[0], ssm_lambda_im[0], ssm_log_dt[0],
                                            ssm_b_re[0], ssm_b_im[0], ssm_c_re[0], ssm_c_im[0])
    lower_bounds = jnp.cumsum(jax.nn.softmax(hgrn_lb_logits.astype(F32), axis=0), axis=0)
    w = {
        "mix_g": mix_norm_g[0].reshape(1, D_MODEL),
        "w_in": w_in[0].astype(BF16),
        "s5_w1": s5_w1, "s5_ms": s5_ms, "s5_c1": s5_c1, "s5_c2": s5_c2,
        "d_skip": ssm_d[0].reshape(1, D_MODEL),
        "w_glu": ssm_w_glu[0].astype(BF16),
        "lb": lower_bounds[0].reshape(1, D_MODEL),
        "hgrn_g": jnp.tile(hgrn_norm_g[0], HGRN_HEADS).reshape(1, D_MODEL),
        "w_a": w_ssm_proj[0].astype(BF16),
        "w_b": w_hgrn_proj[0].astype(BF16),
        "w_out": w_out[0].astype(BF16),
        "ffn_g": ffn_norm_g[0].reshape(1, D_MODEL),
        "w_up": w_up[0].astype(BF16),
        "conv_w": conv_w[0],
        "conv_b": conv_b[0].reshape(1, 2 * D_FF),
        "w_down": w_down[0].astype(BF16),
        "final_g": final_norm_g.reshape(1, D_MODEL),
    }
    zero_states = (
        jnp.zeros((1, 2, SSM_GROUPS, SUBLANES, SSM_STATE_LANES), F32),
        jnp.zeros((1, HGRN_HEADS // HGRN_HEADS_PER_STEP, SUBLANES, HGRN_HEAD_DIM,
                   HGRN_HEADS_PER_STEP * HGRN_HEAD_DIM), F32),
        jnp.zeros((1, 2 * N_FF_CHUNKS, (CONV_WIDTH - 1) * SUBLANES, FF_CHUNK), F32),
    )
    meta = jnp.broadcast_to(meta_tokens.astype(x.dtype)[None], (SUBLANES, N_META, D_MODEL))
    _, states = _block(meta, zero_states, w)
    out, _ = _block(x, states, w)
    return out
```

```python
import functools
import math

import numpy as np
import jax
import jax.numpy as jnp
from jax import lax
from jax.experimental import pallas as pl
from jax.experimental.pallas import tpu as pltpu

F32 = jnp.float32
BF16 = jnp.bfloat16

D_MODEL = 1024
N_META = 16
SSM_GROUPS = 64
SSM_GROUP = 16
SSM_STATE = 64
SSM_CHUNK = 16
SSM_FLAT = SSM_CHUNK * SSM_GROUP
SSM_STATE_LANES = 2 * SSM_STATE
HGRN_HEADS = 8
HGRN_HEAD_DIM = 128
HGRN_CHUNK = 32
HGRN_HEADS_PER_STEP = 2
D_FF = 2816
FF_CHUNK = 256
N_FF_CHUNKS = D_FF // FF_CHUNK
CONV_WIDTH = 3
IN_BLOCKS = 7
EPS = 1e-6
GELU_C0 = math.sqrt(2.0 / math.pi)
GELU_C1 = 0.044715 * GELU_C0
LANES = 128
SUBLANES = 8
N_LANE_SLABS = D_MODEL // LANES
GROUPS_PER_SLAB = LANES // SSM_GROUP
VMEM_LIMIT_BYTES = 56 * 1024 * 1024
TILES = (64, 128, 512, 128, 128)


def _rmsnorm(x, g):
    return x * lax.rsqrt(jnp.mean(x * x, axis=-1, keepdims=True) + EPS) * g


def _sigmoid(x):
    return 0.5 * jnp.tanh(0.5 * x) + 0.5


def _silu(x):
    h = 0.5 * x
    return h + h * jnp.tanh(h)


def _gelu_tanh(x):
    h = 0.5 * x
    return h + h * jnp.tanh(x * (GELU_C0 + GELU_C1 * (x * x)))


def _resident(shape):
    n = len(shape)
    return pl.BlockSpec(shape, lambda *_: (0,) * n, pipeline_mode=pl.Buffered(1))


def _params(semantics):
    return pltpu.CompilerParams(dimension_semantics=semantics, vmem_limit_bytes=VMEM_LIMIT_BYTES)


def _rows(tt, cols, k=0):
    return pl.BlockSpec((None, tt * SUBLANES, cols), lambda g, t, k=k: (g, t, k))


def _inproj_kernel(x_ref, g_ref, w_ref, p_ref, h_ref, slab_ref, *, tt):
    for b in range(SUBLANES):
        for j in range(N_LANE_SLABS):
            slab_ref[j, pl.ds(b, tt, stride=SUBLANES), :] = x_ref[b, :, j * LANES:(j + 1) * LANES]
    h = jnp.concatenate([slab_ref[j] for j in range(N_LANE_SLABS)], axis=1)
    h_ref[...] = h
    z = _rmsnorm(h, g_ref[...]).astype(BF16)
    for k in range(IN_BLOCKS):
        cols = slice(k * D_MODEL, (k + 1) * D_MODEL)
        p_ref[:, cols] = jnp.dot(z, w_ref[:, cols], preferred_element_type=F32).astype(BF16)


def _inproj(x, g, w, tt):
    b, l, _ = x.shape
    ng = b // SUBLANES
    return pl.pallas_call(
        functools.partial(_inproj_kernel, tt=tt),
        grid=(ng, l // tt),
        in_specs=[
            pl.BlockSpec((SUBLANES, tt, D_MODEL), lambda i, t: (i, t, 0)),
            _resident((1, D_MODEL)),
            _resident((D_MODEL, IN_BLOCKS * D_MODEL)),
        ],
        out_specs=[_rows(tt, IN_BLOCKS * D_MODEL), _rows(tt, D_MODEL)],
        out_shape=[
            jax.ShapeDtypeStruct((ng, l * SUBLANES, IN_BLOCKS * D_MODEL), BF16),
            jax.ShapeDtypeStruct((ng, l * SUBLANES, D_MODEL), F32),
        ],
        scratch_shapes=[pltpu.VMEM((N_LANE_SLABS, tt * SUBLANES, LANES), F32)],
        compiler_params=_params(("parallel", "parallel")),
        name="inproj",
    )(x, g, w)


def _s5_kernel(u_ref, w1_ref, ms_ref, c1_ref, c2_ref, d_ref, wglu_ref, s0_ref,
               ya_ref, sfin_ref, st_ref, z_ref, o_ref, y_ref, *, tt):
    ti = pl.program_id(1)

    @pl.when(ti == 0)
    def _():
        st_ref[...] = s0_ref[...]

    nc = tt // SSM_CHUNK
    m = nc * SUBLANES
    chunk_rows = SSM_CHUNK * SUBLANES
    half = SSM_CHUNK // 2
    lane_slot = lax.broadcasted_iota(jnp.int32, (m, LANES), 1) // SSM_GROUP
    in_slot = [lane_slot == s for s in range(GROUPS_PER_SLAB)]

    u = u_ref[...].astype(F32)

    def step_rows(x, t, j):
        return jnp.concatenate(
            [x[c * chunk_rows + t * SUBLANES:c * chunk_rows + (t + 1) * SUBLANES, j * LANES:(j + 1) * LANES]
             for c in range(nc)], axis=0)

    def slab_outputs(j):
        groups = range(j * GROUPS_PER_SLAB, (j + 1) * GROUPS_PER_SLAB)
        gs = slice(j * GROUPS_PER_SLAB, (j + 1) * GROUPS_PER_SLAB)
        rs = [jnp.dot(z_ref[g], w1_ref[g], preferred_element_type=F32) for g in groups]
        s, sw = st_ref[0, gs], st_ref[1, gs]
        c1, c2 = c1_ref[gs], c2_ref[gs]
        carried = []
        for c in range(nc):
            carried.append(s)
            rows_c = slice(c * SUBLANES, (c + 1) * SUBLANES)
            inc = jnp.stack([r[rows_c, SSM_FLAT:SSM_FLAT + SSM_STATE_LANES] for r in rs])
            inc_sw = jnp.stack([r[rows_c, SSM_FLAT + SSM_STATE_LANES:] for r in rs])
            s, sw = c1 * s + c2 * sw + inc, c1 * sw - c2 * s + inc_sw
        st_ref[0, gs] = s
        st_ref[1, gs] = sw
        for i, g in enumerate(groups):
            before = jnp.concatenate([cs[i] for cs in carried], axis=0).astype(BF16)
            o_ref[g] = rs[i][:, :SSM_FLAT] + jnp.dot(before, ms_ref[g], preferred_element_type=F32)

    for j in range(N_LANE_SLABS):
        rot = []
        for t in range(SSM_CHUNK):
            k = t % half
            x = step_rows(u, t, j)
            rot.append(x if k == 0 else pltpu.roll(x, SSM_GROUP * k, axis=1))
        for gg in range(GROUPS_PER_SLAB):
            for hf in range(2):
                acc = rot[half * hf]
                for k in range(1, half):
                    acc = jnp.where(in_slot[(gg + k) % half], rot[half * hf + k], acc)
                z_ref[j * GROUPS_PER_SLAB + gg, :, hf * LANES:(hf + 1) * LANES] = acc.astype(BF16)

    for j in range(N_LANE_SLABS):
        slab_outputs(j)

    for j in range(N_LANE_SLABS):
        for hf in range(2):
            ys = [o_ref[j * GROUPS_PER_SLAB + gg, :, hf * LANES:(hf + 1) * LANES]
                  for gg in range(GROUPS_PER_SLAB)]
            for k in range(half):
                acc = ys[0]
                for gg in range(1, GROUPS_PER_SLAB):
                    acc = jnp.where(in_slot[(gg + k) % half], ys[gg], acc)
                nat = acc if k == 0 else pltpu.roll(acc, LANES - SSM_GROUP * k, axis=1)
                t = half * hf + k
                for c in range(nc):
                    y_ref[c * chunk_rows + t * SUBLANES:c * chunk_rows + (t + 1) * SUBLANES,
                          j * LANES:(j + 1) * LANES] = nat[c * SUBLANES:(c + 1) * SUBLANES]

    y = _gelu_tanh(y_ref[...] + d_ref[...] * u)
    y = y.astype(BF16)
    gate = jnp.dot(y, wglu_ref[...], preferred_element_type=F32)
    ya_ref[...] = y * _sigmoid(gate.astype(BF16))

    @pl.when(ti == pl.num_programs(1) - 1)
    def _():
        sfin_ref[...] = st_ref[...]


def _s5(p, w1, ms, c1, c2, d_skip, w_glu, s0, tt):
    ng, rows, _ = p.shape
    m = tt // SSM_CHUNK * SUBLANES
    state_shape = (2, SSM_GROUPS, SUBLANES, SSM_STATE_LANES)
    return pl.pallas_call(
        functools.partial(_s5_kernel, tt=tt),
        grid=(ng, rows // (tt * SUBLANES)),
        in_specs=[
            _rows(tt, D_MODEL, 0),
            _resident(w1.shape),
            _resident(ms.shape),
            _resident(c1.shape),
            _resident(c2.shape),
            _resident((1, D_MODEL)),
            _resident((D_MODEL, D_MODEL)),
            pl.BlockSpec((None,) + state_shape, lambda g, t: (0, 0, 0, 0, 0)),
        ],
        out_specs=[
            _rows(tt, D_MODEL),
            pl.BlockSpec((None,) + state_shape, lambda g, t: (g, 0, 0, 0, 0)),
        ],
        out_shape=[
            jax.ShapeDtypeStruct((ng, rows, D_MODEL), BF16),
            jax.ShapeDtypeStruct((ng,) + state_shape, F32),
        ],
        scratch_shapes=[
            pltpu.VMEM(state_shape, F32),
            pltpu.VMEM((SSM_GROUPS, m, SSM_FLAT), BF16),
            pltpu.VMEM((SSM_GROUPS, m, SSM_FLAT), F32),
            pltpu.VMEM((tt * SUBLANES, D_MODEL), F32),
        ],
        compiler_params=_params(("parallel", "arbitrary")),
        name="s5",
    )(p, w1, ms, c1, c2, d_skip, w_glu, s0)


def _hgrn_kernel(q_ref, f_ref, i_ref, og_ref, lb_ref, ng_ref, s0_ref,
                 yb_ref, sfin_ref, st_ref, qin_ref, kout_ref, v_ref, oint_ref, *, tt, chunk, heads):
    ti = pl.program_id(2)

    @pl.when(ti == 0)
    def _():
        st_ref[...] = s0_ref[...]

    rows = chunk * SUBLANES
    r_idx = lax.broadcasted_iota(jnp.int32, (rows, rows), 0)
    c_idx = lax.broadcasted_iota(jnp.int32, (rows, rows), 1)
    allowed = (((r_idx - c_idx) & (SUBLANES - 1)) == 0) & (c_idx <= r_idx)
    mid = chunk // 2 - 1
    head = (chunk, SUBLANES, HGRN_HEAD_DIM)

    def decays(hh, ci):
        r = slice(ci * rows, (ci + 1) * rows)
        slot = ci % 2
        lanes = slice(hh * HGRN_HEAD_DIM, (hh + 1) * HGRN_HEAD_DIM)
        half_gap = 0.5 * (1.0 - lb_ref[:, lanes])
        swing = half_gap * jnp.tanh(0.5 * f_ref[r, lanes].astype(F32))
        f = (1.0 - half_gap) + swing
        k = (half_gap - swing).reshape(head)
        cum = jnp.log(f).reshape(head)
        shift = 1
        while shift < chunk:
            cum = cum + jnp.concatenate([jnp.zeros((shift,) + head[1:], F32), cum[:-shift]], axis=0)
            shift *= 2
        ref_row = cum[mid:mid + 1]
        last = cum[chunk - 1:chunk]
        q = q_ref[r, lanes].astype(F32).reshape(head)
        q_mid = q * jnp.exp(cum - ref_row)
        k_mid = k * jnp.exp(ref_row - cum)
        qin_ref[slot, hh] = (q_mid * jnp.exp(ref_row)).reshape(rows, HGRN_HEAD_DIM)
        kout_ref[slot, hh] = (k_mid * jnp.exp(last - ref_row)).reshape(rows, HGRN_HEAD_DIM)
        v = i_ref[r, lanes]
        v_ref[slot, hh] = v.astype(F32)
        return (q_mid.reshape(rows, HGRN_HEAD_DIM).astype(BF16),
                k_mid.reshape(rows, HGRN_HEAD_DIM).astype(BF16), v, jnp.exp(last[0]))

    rb =[pl.ds(b, chunk, stride=SUBLANES) for b in range(SUBLANES)]

    def matmuls(ci, operands):
        r = slice(ci * rows, (ci + 1) * rows)
        slot = ci % 2
        q_mid, k_mid, v, decay = zip(*operands)
        scores = [lax.dot_general(q_mid[hh], k_mid[hh], (((1,), (1,)), ((), ())),
                                  preferred_element_type=F32) for hh in range(heads)]
        scores = [jnp.where(allowed, s, 0.0).astype(BF16) for s in scores]
        o = [jnp.dot(scores[hh], v[hh], preferred_element_type=F32) for hh in range(heads)]
        no_keys = jnp.zeros((chunk, HGRN_HEAD_DIM), BF16)
        for b in range(SUBLANES):
            state_t = st_ref[b]
            state_bf = state_t.astype(BF16)
            values, keys = [], []
            for hh in range(heads):
                lanes = slice(hh * HGRN_HEAD_DIM, (hh + 1) * HGRN_HEAD_DIM)
                oint_ref[slot, hh, rb[b], :] = lax.dot_general(
                    qin_ref[slot, hh, rb[b], :].astype(BF16), state_bf[:, lanes],
                    (((1,), (1,)), ((), ())), preferred_element_type=F32)
                values.append(v_ref[slot, hh, rb[b], :].astype(BF16))
                k_out = kout_ref[slot, hh, rb[b], :].astype(BF16)
                keys.append(jnp.concatenate([k_out if other == hh else no_keys for other in range(heads)],
                                            axis=1))
            update = lax.dot_general(jnp.concatenate(values, axis=0), jnp.concatenate(keys, axis=0),
                                     (((0,), (0,)), ((), ())), preferred_element_type=F32)
            decay_b = jnp.concatenate([decay[hh][b:b + 1, :] for hh in range(heads)], axis=1)
            st_ref[b] = state_t * decay_b + update
        for hh in range(heads):
            lanes = slice(hh * HGRN_HEAD_DIM, (hh + 1) * HGRN_HEAD_DIM)
            out = _rmsnorm(o[hh] + oint_ref[slot, hh], ng_ref[:, lanes])
            yb_ref[r, lanes] = (out * _silu(og_ref[r, lanes].astype(F32))).astype(BF16)

    n_chunks = tt // chunk
    operands = [decays(hh, 0) for hh in range(heads)]
    for ci in range(n_chunks):
        following = [decays(hh, ci + 1) for hh in range(heads)] if ci + 1 < n_chunks else None
        matmuls(ci, operands)
        operands = following

    @pl.when(ti == pl.num_programs(2) - 1)
    def _():
        sfin_ref[...] = st_ref[...]


def _hgrn(p, lb, ng_, s0, tt, chunk):
    ng, rows, _ = p.shape
    heads = HGRN_HEADS_PER_STEP
    width = heads * HGRN_HEAD_DIM
    steps = HGRN_HEADS // heads
    col = lambda k: pl.BlockSpec((None, tt * SUBLANES, width), lambda g, h, t, k=k: (g, t, k * steps + h))
    per_head = pl.BlockSpec((1, width), lambda g, h, t: (0, h))
    state_shape = (None, SUBLANES, HGRN_HEAD_DIM, width)
    chunk_buf = pltpu.VMEM((2, heads, chunk * SUBLANES, HGRN_HEAD_DIM), F32)
    return pl.pallas_call(
        functools.partial(_hgrn_kernel, tt=tt, chunk=chunk, heads=heads),
        grid=(ng, steps, rows // (tt * SUBLANES)),
        in_specs=[col(1), col(2), col(3), col(4), per_head, per_head,
                  pl.BlockSpec((None,) + state_shape, lambda g, h, t: (0, h, 0, 0, 0))],
        out_specs=[
            pl.BlockSpec((None, tt * SUBLANES, width), lambda g, h, t: (g, t, h)),
            pl.BlockSpec((None,) + state_shape, lambda g, h, t: (g, h, 0, 0, 0)),
        ],
        out_shape=[
            jax.ShapeDtypeStruct((ng, rows, D_MODEL), BF16),
            jax.ShapeDtypeStruct((ng, steps) + state_shape[1:], F32),
        ],
        scratch_shapes=[pltpu.VMEM(state_shape[1:], F32), chunk_buf, chunk_buf, chunk_buf, chunk_buf],
        compiler_params=_params(("parallel", "parallel", "arbitrary")),
        name="hgrn2",
    )(p, p, p, p, lb, ng_, s0)


def _merge_kernel(h_ref, ga_ref, gb_ref, ya_ref, yb_ref, wa_ref, wb_ref, wo_ref, o_ref):
    a = jnp.dot(ya_ref[...], wa_ref[...], preferred_element_type=F32)
    b = jnp.dot(yb_ref[...], wb_ref[...], preferred_element_type=F32)
    merged = _sigmoid(ga_ref[...].astype(F32)) * a + _sigmoid(gb_ref[...].astype(F32)) * b
    o_ref[...] = h_ref[...] + jnp.dot(merged.astype(BF16), wo_ref[...], preferred_element_type=F32)


def _merge(h, p, ya, yb, wa, wb, wo, tt):
    ng, rows, _ = h.shape
    w = _resident((D_MODEL, D_MODEL))
    return pl.pallas_call(
        _merge_kernel,
        grid=(ng, rows // (tt * SUBLANES)),
        in_specs=[_rows(tt, D_MODEL), _rows(tt, D_MODEL, 5), _rows(tt, D_MODEL, 6),
                  _rows(tt, D_MODEL), _rows(tt, D_MODEL), w, w, w],
        out_specs=_rows(tt, D_MODEL),
        out_shape=jax.ShapeDtypeStruct((ng, rows, D_MODEL), F32),
        compiler_params=_params(("parallel", "parallel")),
        name="merge",
    )(h, p, p, ya, yb, wa, wb, wo)


def _ffn_kernel(h_ref, g_ref, wu_ref, cw_ref, cb_ref, wd_ref, gf_ref,
                c0_ref, o_ref, cfin_ref, carry_ref, act_ref, slab_ref, *, tt):
    ti = pl.program_id(1)

    @pl.when(ti == 0)
    def _():
        carry_ref[...] = c0_ref[...]

    rows = tt * SUBLANES
    tail = (CONV_WIDTH - 1) * SUBLANES
    h = h_ref[...]
    z = _rmsnorm(h, g_ref[...]).astype(BF16)

    def conv_up(n):
        cols = slice(n * FF_CHUNK, (n + 1) * FF_CHUNK)
        pre = jnp.dot(z, wu_ref[:, cols], preferred_element_type=F32)
        ext = jnp.concatenate([carry_ref[n], pre], axis=0)
        carry_ref[n] = ext[rows:rows + tail]
        out = cb_ref[:, cols]
        for j in range(CONV_WIDTH):
            out = out + ext[j * SUBLANES:j * SUBLANES + rows] * cw_ref[j:j + 1, cols]
        return out

    for n in range(N_FF_CHUNKS):
        gate = conv_up(n)
        val = conv_up(N_FF_CHUNKS + n)
        act_ref[:, n * FF_CHUNK:(n + 1) * FF_CHUNK] = (_silu(gate) * val).astype(BF16)

    down = jnp.dot(act_ref[...], wd_ref[...], preferred_element_type=F32)
    out = _rmsnorm(h + down, gf_ref[...])
    for j in range(N_LANE_SLABS):
        slab_ref[j] = out[:, j * LANES:(j + 1) * LANES]
    for b in range(SUBLANES):
        for j in range(N_LANE_SLABS):
            o_ref[b, :, j * LANES:(j + 1) * LANES] = slab_ref[j, pl.ds(b, tt, stride=SUBLANES), :]

    @pl.when(ti == pl.num_programs(1) - 1)
    def _():
        cfin_ref[...] = carry_ref[...]


def _ffn(h, g, wu, cw, cb, wd, gf, c0, tt):
    ng, rows, _ = h.shape
    l = rows // SUBLANES
    carry_shape = (2 * N_FF_CHUNKS, (CONV_WIDTH - 1) * SUBLANES, FF_CHUNK)
    return pl.pallas_call(
        functools.partial(_ffn_kernel, tt=tt),
        grid=(ng, l // tt),
        in_specs=[
            _rows(tt, D_MODEL),
            _resident((1, D_MODEL)),
            _resident(wu.shape), _resident(cw.shape), _resident(cb.shape), _resident(wd.shape),
            _resident((1, D_MODEL)),
            pl.BlockSpec((None,) + carry_shape, lambda g, t: (0, 0, 0, 0)),
        ],
        out_specs=[
            pl.BlockSpec((SUBLANES, tt, D_MODEL), lambda g, t: (g, t, 0)),
            pl.BlockSpec((None,) + carry_shape, lambda g, t: (g, 0, 0, 0)),
        ],
        out_shape=[
            jax.ShapeDtypeStruct((ng * SUBLANES, l, D_MODEL), F32),
            jax.ShapeDtypeStruct((ng,) + carry_shape, F32),
        ],
        scratch_shapes=[
            pltpu.VMEM(carry_shape, F32),
            pltpu.VMEM((tt * SUBLANES, D_FF), BF16),
            pltpu.VMEM((N_LANE_SLABS, tt * SUBLANES, LANES), F32),
        ],
        compiler_params=_params(("parallel", "arbitrary")),
        name="ffn",
    )(h, g, wu, cw, cb, wd, gf, c0)


def _s5_params(lam_re, lam_im, log_dt, b_re, b_im, c_re, c_im):
    hp = lax.Precision.HIGHEST
    t_len = SSM_CHUNK
    dt = jnp.exp(log_dt)[:, None]
    mag = jnp.exp(lam_re * dt)
    ab_re = mag * jnp.cos(lam_im * dt)
    ab_im = mag * jnp.sin(lam_im * dt)
    den = lam_re * lam_re + lam_im * lam_im
    nr = ab_re - 1.0
    coef_re = (nr * lam_re + ab_im * lam_im) / den
    coef_im = (ab_im * lam_re - nr * lam_im) / den
    bb_re = coef_re[..., None] * b_re - coef_im[..., None] * b_im
    bb_im = coef_re[..., None] * b_im + coef_im[..., None] * b_re
    tau = jnp.arange(t_len + 1, dtype=F32)[None, :, None]
    pw_mag = jnp.exp((lam_re * dt)[:, None, :] * tau)
    pw_re = pw_mag * jnp.cos((lam_im * dt)[:, None, :] * tau)
    pw_im = pw_mag * jnp.sin((lam_im * dt)[:, None, :] * tau)
    ab_b_re = pw_re[..., None] * bb_re[:, None] - pw_im[..., None] * bb_im[:, None]
    ab_b_im = pw_re[..., None] * bb_im[:, None] + pw_im[..., None] * bb_re[:, None]
    kern = (jnp.einsum("gop,gtpi->gtoi", c_re, ab_b_re[:, :t_len], precision=hp)
            - jnp.einsum("gop,gtpi->gtoi", c_im, ab_b_im[:, :t_len], precision=hp))
    out_re = c_re[:, None] * pw_re[:, :, None, :] - c_im[:, None] * pw_im[:, :, None, :]
    out_im = -(c_re[:, None] * pw_im[:, :, None, :] + c_im[:, None] * pw_re[:, :, None, :])

    half = t_len // 2
    slots = np.arange(t_len)
    step_of = np.stack([half * (slots // half) + (slots % half - gg) % half
                        for gg in range(GROUPS_PER_SLAB)])
    lags = np.arange(t_len + 1)
    inc_sel = (t_len - 1 - step_of[:, :, None]) == lags[None, None, :t_len]
    out_sel = (step_of[:, :, None] + 1) == lags[None, None, :]
    by_slab = lambda a: a.astype(BF16).reshape((N_LANE_SLABS, GROUPS_PER_SLAB) + a.shape[1:])
    select = lambda spec, sel, a: jnp.einsum(spec, jnp.asarray(sel, BF16), by_slab(a),
                                             preferred_element_type=F32).astype(BF16)
    dims = (GROUPS_PER_SLAB, SSM_FLAT, t_len, SSM_FLAT)
    x_i, k_i, r_i, c_i = (lax.broadcasted_iota(jnp.int32, dims, d) for d in range(4))
    step_at = lambda s: half * (s // half) + (s % half - x_i) % half
    lag_eye = ((step_at(c_i // SSM_GROUP) - step_at(r_i) == k_i // SSM_GROUP)
               & (k_i % SSM_GROUP == c_i % SSM_GROUP)).astype(BF16)
    kern_rows = by_slab(kern.transpose(0, 3, 1, 2).reshape(SSM_GROUPS, SSM_GROUP, SSM_FLAT))
    toep = jnp.einsum("jxik,xkrC->jxriC", kern_rows, lag_eye, preferred_element_type=F32)
    toep = toep.astype(BF16).reshape(SSM_GROUPS, SSM_FLAT, SSM_FLAT)
    inc_re = select("xsu,jxuph->jxshp", inc_sel, ab_b_re[:, :t_len])
    inc_im = select("xsu,jxuph->jxshp", inc_sel, ab_b_im[:, :t_len])
    inc = jnp.concatenate([inc_re, inc_im, inc_im, inc_re], axis=-1)
    w1 = jnp.concatenate([toep, inc.reshape(SSM_GROUPS, SSM_FLAT, 2 * SSM_STATE_LANES)], axis=-1)
    ms = jnp.concatenate([select("xsu,jxuop->jxpso", out_sel, out_re),
                          select("xsu,jxuop->jxpso", out_sel, out_im)], axis=2)
    ms = ms.reshape(SSM_GROUPS, SSM_STATE_LANES, SSM_FLAT)
    a16_re, a16_im = pw_re[:, t_len], pw_im[:, t_len]
    c1 = jnp.concatenate([a16_re, a16_re], axis=-1)
    c2 = jnp.concatenate([-a16_im, a16_im], axis=-1)
    bcast = lambda a: jnp.broadcast_to(a[:, None, :], (SSM_GROUPS, SUBLANES, SSM_STATE_LANES))
    return w1, ms, bcast(c1), bcast(c2)


def _block(x, states, w):
    l = x.shape[1]
    s5_state, hgrn_state, conv_carry = states
    t_in, t_s5, t_hgrn, t_merge, t_ffn = (min(t, l) for t in TILES)
    p, h = _inproj(x, w["mix_g"], w["w_in"], t_in)
    ya, s5_state = _s5(p, w["s5_w1"], w["s5_ms"], w["s5_c1"], w["s5_c2"], w["d_skip"], w["w_glu"],
                       s5_state, t_s5)
    yb, hgrn_state = _hgrn(p, w["lb"], w["hgrn_g"], hgrn_state, t_hgrn, min(HGRN_CHUNK, l))
    h1 = _merge(h, p, ya, yb, w["w_a"], w["w_b"], w["w_out"], t_merge)
    out, conv_carry = _ffn(h1, w["ffn_g"], w["w_up"], w["conv_w"], w["conv_b"], w["w_down"],
                           w["final_g"], conv_carry, t_ffn)
    return out, (s5_state, hgrn_state, conv_carry)


def kernel(x, meta_tokens, mix_norm_g, w_in, ssm_lambda_re, ssm_lambda_im, ssm_log_dt, ssm_b_re, ssm_b_im, ssm_c_re, ssm_c_im, ssm_d, ssm_w_glu, w_ssm_proj, hgrn_lb_logits, hgrn_norm_g, w_hgrn_proj, w_out, ffn_norm_g, w_up, conv_w, conv_b, w_down, final_norm_g):
    assert x.shape[0] % SUBLANES == 0 and x.shape[1] % SSM_CHUNK == 0 and x.shape[2] == D_MODEL
    s5_w1, s5_ms, s5_c1, s5_c2 = _s5_params(ssm_lambda_re[0], ssm_lambda_im[0], ssm_log_dt[0],
                                            ssm_b_re[0], ssm_b_im[0], ssm_c_re[0], ssm_c_im[0])
    lower_bounds = jnp.cumsum(jax.nn.softmax(hgrn_lb_logits.astype(F32), axis=0), axis=0)
    w = {
        "mix_g": mix_norm_g[0].reshape(1, D_MODEL),
        "w_in": w_in[0].astype(BF16),
        "s5_w1": s5_w1, "s5_ms": s5_ms, "s5_c1": s5_c1, "s5_c2": s5_c2,
        "d_skip": ssm_d[0].reshape(1, D_MODEL),
        "w_glu": ssm_w_glu[0].astype(BF16),
        "lb": lower_bounds[0].reshape(1, D_MODEL),
        "hgrn_g": jnp.tile(hgrn_norm_g[0], HGRN_HEADS).reshape(1, D_MODEL),
        "w_a": w_ssm_proj[0].astype(BF16),
        "w_b": w_hgrn_proj[0].astype(BF16),
        "w_out": w_out[0].astype(BF16),
        "ffn_g": ffn_norm_g[0].reshape(1, D_MODEL),
        "w_up": w_up[0].astype(BF16),
        "conv_w": conv_w[0],
        "conv_b": conv_b[0].reshape(1, 2 * D_FF),
        "w_down": w_down[0].astype(BF16),
        "final_g": final_norm_g.reshape(1, D_MODEL),
    }
    zero_states = (
        jnp.zeros((1, 2, SSM_GROUPS, SUBLANES, SSM_STATE_LANES), F32),
        jnp.zeros((1, HGRN_HEADS // HGRN_HEADS_PER_STEP, SUBLANES, HGRN_HEAD_DIM,
                   HGRN_HEADS_PER_STEP * HGRN_HEAD_DIM), F32),
        jnp.zeros((1, 2 * N_FF_CHUNKS, (CONV_WIDTH - 1) * SUBLANES, FF_CHUNK), F32),
    )
    meta = jnp.broadcast_to(meta_tokens.astype(x.dtype)[None], (SUBLANES, N_META, D_MODEL))
    _, states = _block(meta, zero_states, w)
    out, _ = _block(x, states, w)
    return out
```
